```python
import math
import jax, jax.numpy as jnp
from jax import lax
import numpy as np

D_MODEL = 1024
BATCH = 2
SEQ = 8192
DEPTH = 2
DEC_BATCH = 128
DEC_SEQ = 4
PAST_LEN = 2048
PAGE_SIZE = 128

N_A_LAYERS = DEPTH // 2
N_B_LAYERS = DEPTH - N_A_LAYERS
CHUNK = 128
GATE = 2 * D_MODEL
SGU_GROUPS = 8
SGU_GROUP_DIM = GATE // SGU_GROUPS
WINDOWS = (128, 512, 2048)
DILATIONS = (1, 4, 16)
N_BRANCH = len(WINDOWS)
HEAD_DIM = 64
N_HEADS = D_MODEL // HEAD_DIM
D_FF = 4 * D_MODEL
NUM_BUCKETS = 32
MAX_DISTANCE = max(WINDOWS)
Q_BLOCK = 128
ALPHA = (2 * DEPTH) ** 0.25
BETA = (8 * DEPTH) ** -0.25
LN_EPS = 1e-5
NEG = -1e30

kernel_name = "yoco_gmlp_dilated_swa_step"


def layernorm(x, g, b):
    xf = x.astype(jnp.float32)
    mu = jnp.mean(xf, axis=-1, keepdims=True)
    var = jnp.mean(jnp.square(xf - mu), axis=-1, keepdims=True)
    y = (xf - mu) * lax.rsqrt(var + LN_EPS) * g.astype(jnp.float32) + b.astype(jnp.float32)
    return y.astype(x.dtype)


def sq_relu_mlp(x, w1, w2):
    return jnp.square(jax.nn.relu(x @ w1)) @ w2


def t5_bucket(dist):
    max_exact = NUM_BUCKETS // 2
    n = jnp.maximum(dist, 1).astype(jnp.float32)
    large = max_exact + (jnp.log(n / max_exact) / math.log(MAX_DISTANCE / max_exact)
                         * (NUM_BUCKETS - max_exact)).astype(jnp.int32)
    large = jnp.minimum(large, NUM_BUCKETS - 1)
    return jnp.where(dist < max_exact, dist, large)


def branch_biases(rel_bias):
    out = []
    for g in range(N_BRANCH):
        n_keys = WINDOWS[g] // DILATIONS[g] + 1
        dist = jnp.arange(n_keys, dtype=jnp.int32) * DILATIONS[g]
        b = rel_bias[t5_bucket(dist)][:, g * N_HEADS:(g + 1) * N_HEADS]
        out.append(b.T.astype(jnp.float32))
    return out


def gmlp_mixer(x, w_in, b_in, ln_gv, ln_bv, w_s, b_s, w_out, b_out):
    B, T, _ = x.shape
    z = jax.nn.gelu(x @ w_in + b_in)
    u, v = jnp.split(z, 2, axis=-1)
    vn = layernorm(v, ln_gv, ln_bv)
    n_chunks = -(-T // CHUNK)
    pad = n_chunks * CHUNK - T
    vc = jnp.pad(vn, ((0, 0), (0, pad), (0, 0))).reshape(B, n_chunks, CHUNK, SGU_GROUPS, SGU_GROUP_DIM)
    w_causal = w_s * jnp.tril(jnp.ones((CHUNK, CHUNK), w_s.dtype))
    s = jnp.einsum('gts,bcsgk->bctgk', w_causal, vc) + b_s.T[:, :, None]
    s = s.reshape(B, n_chunks * CHUNK, GATE)[:, :T]
    y = (u * s) @ w_out + b_out
    return y, vn


def dilated_branch(q, slab, q_pos, slab_start, dil, window, bias):
    n_keys = window // dil + 1
    steps = jnp.arange(n_keys, dtype=jnp.int32) * dil
    key_pos = q_pos[:, None] - steps[None, :]
    valid = key_pos >= 0
    idx = jnp.clip(key_pos - slab_start, 0, slab.shape[1] - 1)
    kv = jnp.take(slab, idx, axis=1)
    logits = jnp.einsum('bthc,btkhc->bhtk', q, kv[:, :, :, 0]).astype(jnp.float32)
    logits = logits * (HEAD_DIM ** -0.5) + bias[:, None, :]
    logits = jnp.where(valid, logits, NEG)
    m = jnp.max(logits, axis=-1, keepdims=True)
    p = jnp.exp(logits - m)
    den = jnp.sum(p, axis=-1, keepdims=True)
    o = jnp.einsum('bhtk,btkhc->bthc', p / den, kv[:, :, :, 1].astype(jnp.float32))
    lse = (m + jnp.log(den))[..., 0].transpose(0, 2, 1)
    return o, lse


def mix_branches(q, slabs, q_pos, slab_starts, biases, w_o):
    outs, lses = [], []
    for g in range(N_BRANCH):
        o, l = dilated_branch(q[:, :, g], slabs[g], q_pos, slab_starts[g], DILATIONS[g], WINDOWS[g], biases[g])
        outs.append(o)
        lses.append(l)
    w = jax.nn.softmax(jnp.stack(lses), axis=0)
    o = jnp.sum(w[..., None] * jnp.stack(outs), axis=0)
    B, T = o.shape[:2]
    return o.reshape(B, T, N_HEADS * HEAD_DIM).astype(q.dtype) @ w_o


def dilated_prompt(q, kv, w_o, biases):
    B, S = q.shape[:2]
    kv_pads = [jnp.pad(kv[:, :, g], ((0, 0), (WINDOWS[g], 0), (0, 0), (0, 0), (0, 0))) for g in range(N_BRANCH)]

    def block(i):
        start = i * Q_BLOCK
        qb = lax.dynamic_slice_in_dim(q, start, Q_BLOCK, axis=1)
        q_pos = start + jnp.arange(Q_BLOCK, dtype=jnp.int32)
        slabs = [lax.dynamic_slice_in_dim(kv_pads[g], start, WINDOWS[g] + Q_BLOCK, axis=1) for g in range(N_BRANCH)]
        starts = [start - WINDOWS[g] for g in range(N_BRANCH)]
        return mix_branches(qb, slabs, q_pos, starts, biases, w_o)

    out = lax.map(block, jnp.arange(S // Q_BLOCK, dtype=jnp.int32))
    return out.transpose(1, 0, 2, 3).reshape(B, S, out.shape[-1])


def dilated_sample(q, kv, caches, w_o, biases):
    T = q.shape[1]
    q_pos = PAST_LEN + jnp.arange(T, dtype=jnp.int32)
    slabs = [jnp.concatenate([caches[g].astype(kv.dtype), kv[:, :, g]], axis=1) for g in range(N_BRANCH)]
    starts = [PAST_LEN - caches[g].shape[1] for g in range(N_BRANCH)]
    return mix_branches(q, slabs, q_pos, starts, biases, w_o)


def trunk(x, caches, ln_g, ln_b, gm_w_in, gm_b_in, gm_ln_g, gm_ln_b, gm_w_s, gm_b_s, gm_w_out, gm_b_out,
          w_kv, attn_w_q, attn_w_o, rel_bias, mlp_w1, mlp_w2):
    T = x.shape[1]
    chunk_start = ((T - 1) // CHUNK) * CHUNK
    biases = branch_biases(rel_bias)
    v_rows = []
    kv = None
    for i in range(DEPTH):
        if i < N_A_LAYERS:
            y, vn = gmlp_mixer(x, gm_w_in[i], gm_b_in[i], gm_ln_g[i], gm_ln_b[i], gm_w_s[i], gm_b_s[i],
                               gm_w_out[i], gm_b_out[i])
            v_rows.append(vn[:, chunk_start:])
        else:
            j = i - N_A_LAYERS
            q = jnp.einsum('btd,dghc->btghc', x, attn_w_q[j])
            if caches is None:
                y = dilated_prompt(q, kv, attn_w_o[j], biases)
            else:
                y = dilated_sample(q, kv, caches, attn_w_o[j], biases)
        x = layernorm(ALPHA * x + y, ln_g[i, 0], ln_b[i, 0])
        x = layernorm(ALPHA * x + sq_relu_mlp(x, mlp_w1[i], mlp_w2[i]), ln_g[i, 1], ln_b[i, 1])
        if i == N_A_LAYERS - 1:
            kv = jnp.einsum('btd,dgehc->btgehc', x, w_kv)
    return x, jnp.stack(v_rows), kv


def setup_inputs(seed: int = 0) -> dict:
    key = jax.random.key(seed)
    ks = jax.random.split(key, 24)
    nrm = lambda k, shape: jax.random.normal(k, shape, jnp.float32)
    d = {}
    d["x_prompt"] = nrm(ks[0], (BATCH, SEQ, D_MODEL))
    d["x_sample"] = nrm(ks[1], (DEC_BATCH, DEC_SEQ, D_MODEL))
    for g, name in enumerate(["cache_kv_w128", "cache_kv_w512", "cache_kv_w2048"]):
        d[name] = nrm(ks[2 + g], (DEC_BATCH, min(WINDOWS[g], PAST_LEN), 2, N_HEADS, HEAD_DIM))
    d["ln_g"] = 1.0 + 0.05 * nrm(ks[5], (DEPTH, 2, D_MODEL))
    d["ln_b"] = 0.05 * nrm(ks[6], (DEPTH, 2, D_MODEL))
    d["gm_w_in"] = nrm(ks[7], (N_A_LAYERS, D_MODEL, 2 * GATE)) * D_MODEL ** -0.5
    d["gm_b_in"] = 0.02 * nrm(ks[8], (N_A_LAYERS, 2 * GATE))
    d["gm_ln_g"] = 1.0 + 0.05 * nrm(ks[9], (N_A_LAYERS, GATE))
    d["gm_ln_b"] = 0.05 * nrm(ks[10], (N_A_LAYERS, GATE))
    d["gm_w_s"] = nrm(ks[11], (N_A_LAYERS, SGU_GROUPS, CHUNK, CHUNK)) * CHUNK ** -0.5
    d["gm_b_s"] = 1.0 + 0.05 * nrm(ks[12], (N_A_LAYERS, SGU_GROUPS, CHUNK))
    d["gm_w_out"] = nrm(ks[13], (N_A_LAYERS, GATE, D_MODEL)) * GATE ** -0.5 * BETA
    d["gm_b_out"] = 0.02 * nrm(ks[14], (N_A_LAYERS, D_MODEL))
    kv_scale = jnp.array([1.0, BETA], jnp.float32)[None, None, :, None, None]
    d["w_kv"] = nrm(ks[15], (D_MODEL, N_BRANCH, 2, N_HEADS, HEAD_DIM)) * D_MODEL ** -0.5 * kv_scale
    d["attn_w_q"] = nrm(ks[16], (N_B_LAYERS, D_MODEL, N_BRANCH, N_HEADS, HEAD_DIM)) * D_MODEL ** -0.5
    d["attn_w_o"] = nrm(ks[17], (N_B_LAYERS, N_HEADS * HEAD_DIM, D_MODEL)) * (N_HEADS * HEAD_DIM) ** -0.5 * BETA
    d["rel_bias"] = 0.5 * nrm(ks[18], (NUM_BUCKETS, N_BRANCH * N_HEADS))
    d["mlp_w1"] = nrm(ks[19], (DEPTH, D_MODEL, D_FF)) * D_MODEL ** -0.5
    d["mlp_w2"] = nrm(ks[20], (DEPTH, D_FF, D_MODEL)) * D_FF ** -0.5 * BETA
    return d


def reference(x_prompt, x_sample, cache_kv_w128, cache_kv_w512, cache_kv_w2048, ln_g, ln_b,
              gm_w_in, gm_b_in, gm_ln_g, gm_ln_b, gm_w_s, gm_b_s, gm_w_out, gm_b_out,
              w_kv, attn_w_q, attn_w_o, rel_bias, mlp_w1, mlp_w2):
    weights = (ln_g, ln_b, gm_w_in, gm_b_in, gm_ln_g, gm_ln_b, gm_w_s, gm_b_s, gm_w_out, gm_b_out,
               w_kv, attn_w_q, attn_w_o, rel_bias, mlp_w1, mlp_w2)
    y_prompt, v_prompt, kv_p = trunk(x_prompt, None, *weights)
    caches = (cache_kv_w128, cache_kv_w512, cache_kv_w2048)
    y_sample, v_sample, kv_s = trunk(x_sample, caches, *weights)
    S = x_prompt.shape[1]
    kv128_p = kv_p[:, S - min(WINDOWS[0], S):, 0]
    kv512_p = kv_p[:, S - min(WINDOWS[1], S):, 1]
    kv2048_p = kv_p[:, S - min(WINDOWS[2], S):, 2]
    return (y_prompt, y_sample, v_prompt, v_sample, kv128_p, kv_s[:, :, 0], kv512_p, kv_s[:, :, 1],
            kv2048_p, kv_s[:, :, 2])
```

```python
import functools
import math

import jax
import jax.numpy as jnp
from jax import lax
from jax.experimental import pallas as pl
from jax.experimental.pallas import tpu as pltpu

D_MODEL = 1024
CHUNK = 128
GATE = 2 * D_MODEL
SGU_GROUPS = 8
SGU_GROUP_DIM = GATE // SGU_GROUPS
WINDOWS = (128, 512, 2048)
DILATIONS = (1, 4, 16)
N_BRANCH = 3
HEAD_DIM = 64
N_HEADS = 16
D_FF = 4 * D_MODEL
NUM_BUCKETS = 32
MAX_DISTANCE = 2048
PAST_LEN = 2048
DEPTH = 2
ALPHA = (2 * DEPTH) ** 0.25
LN_EPS = 1e-5
NEG = -1e30
Q_SCALE = HEAD_DIM ** -0.5

BF16 = jnp.bfloat16
F32 = jnp.float32

VMEM_LIMIT_BYTES = 56 * 1024 * 1024
LANES = 128
Q_TILE = 2048
KEY_BLOCK = 128
HEADS_PER_STEP = LANES // HEAD_DIM
N_PAIRS = N_HEADS // HEADS_PER_STEP
NEW_ROWS = 8
QKV_WIDTH = N_BRANCH * 3 * D_MODEL
KV_OFFSET = N_BRANCH * D_MODEL


def _params(n_axes):
    return pltpu.CompilerParams(dimension_semantics=("arbitrary",) * n_axes,
                                vmem_limit_bytes=VMEM_LIMIT_BYTES)


def _const_spec(shape):
    return pl.BlockSpec(shape, lambda *_: (0,) * len(shape), pipeline_mode=pl.Buffered(1))


def _ln(x, g, b):
    mu = jnp.mean(x, axis=-1, keepdims=True)
    xc = x - mu
    var = jnp.mean(xc * xc, axis=-1, keepdims=True)
    return xc * lax.rsqrt(var + LN_EPS) * g + b


def _dot(a, b):
    return jnp.dot(a, b, preferred_element_type=F32)


def _dot_nt(a, b):
    return lax.dot_general(a, b, (((1,), (1,)), ((), ())), preferred_element_type=F32)


def _gmlp_kernel(x_ref, w_in_ref, b_in_ref, gv_ref, bv_ref, ws_ref, bs_ref, w_out_ref, b_out_ref,
                 g_ref, b_ref, x1_ref, vn_ref, vnb_sc, gact_sc, *, tm, blocks_per_seq):
    x = x_ref[...]
    xb = x.astype(BF16)
    zv = _dot(xb, w_in_ref[:, GATE:]) + b_in_ref[:, GATE:]
    vn = _ln(jax.nn.gelu(zv), gv_ref[...], bv_ref[...])
    vnb_sc[...] = vn.astype(BF16)

    if blocks_per_seq is None:
        vn_ref[...] = vn
    else:
        @pl.when(pl.program_id(0) % blocks_per_seq == blocks_per_seq - 1)
        def _():
            vn_ref[0] = vn[tm - CHUNK:, :]

    row = lax.broadcasted_iota(jnp.int32, (CHUNK, CHUNK), 0)
    col = lax.broadcasted_iota(jnp.int32, (CHUNK, CHUNK), 1)
    causal = row >= col
    for g in range(SGU_GROUPS):
        lo, hi = g * SGU_GROUP_DIM, (g + 1) * SGU_GROUP_DIM
        zu = jax.nn.gelu(_dot(xb, w_in_ref[:, lo:hi]) + b_in_ref[:, lo:hi])
        wg = jnp.where(causal, ws_ref[g], 0.0).astype(BF16)
        bsg = bs_ref[:, g:g + 1]
        for c in range(tm // CHUNK):
            r0, r1 = c * CHUNK, (c + 1) * CHUNK
            s = _dot(wg, vnb_sc[r0:r1, lo:hi]) + bsg
            gact_sc[r0:r1, lo:hi] = (zu[r0:r1] * s).astype(BF16)
    y = _dot(gact_sc[...], w_out_ref[...]) + b_out_ref[...]
    x1_ref[...] = _ln(ALPHA * x + y, g_ref[...], b_ref[...])


def _gmlp_layer(x, w_in, b_in, gv, bv, ws, bs_t, w_out, b_out, g, b, *, tm, rows_per_seq):
    m = x.shape[0]
    if rows_per_seq is None:
        blocks_per_seq = None
        vn_shape = jax.ShapeDtypeStruct((m, GATE), F32)
        vn_spec = pl.BlockSpec((tm, GATE), lambda i: (i, 0))
    else:
        blocks_per_seq = rows_per_seq // tm
        vn_shape = jax.ShapeDtypeStruct((m // rows_per_seq, CHUNK, GATE), F32)
        vn_spec = pl.BlockSpec((1, CHUNK, GATE), lambda i: (i // blocks_per_seq, 0, 0))
    return pl.pallas_call(
        functools.partial(_gmlp_kernel, tm=tm, blocks_per_seq=blocks_per_seq),
        grid=(m // tm,),
        in_specs=[
            pl.BlockSpec((tm, D_MODEL), lambda i: (i, 0)),
            _const_spec((D_MODEL, 2 * GATE)), _const_spec((1, 2 * GATE)),
            _const_spec((1, GATE)), _const_spec((1, GATE)),
            _const_spec((SGU_GROUPS, CHUNK, CHUNK)), _const_spec((CHUNK, SGU_GROUPS)),
            _const_spec((GATE, D_MODEL)), _const_spec((1, D_MODEL)),
            _const_spec((1, D_MODEL)), _const_spec((1, D_MODEL)),
        ],
        out_specs=[pl.BlockSpec((tm, D_MODEL), lambda i: (i, 0)), vn_spec],
        out_shape=[jax.ShapeDtypeStruct((m, D_MODEL), F32), vn_shape],
        scratch_shapes=[pltpu.VMEM((tm, GATE), BF16), pltpu.VMEM((tm, GATE), BF16)],
        compiler_params=_params(1),
        name="gmlp_layer",
    )(x, w_in, b_in, gv, bv, ws, bs_t, w_out, b_out, g, b)


def _mlp_body(x, w1_ref, w2_ref, g_ref, b_ref, out_ref):
    xb = x.astype(BF16)
    acc = jnp.zeros_like(x)
    for j in range(D_FF // D_MODEL):
        lo, hi = j * D_MODEL, (j + 1) * D_MODEL
        h = jnp.square(jnp.maximum(_dot(xb, w1_ref[:, lo:hi]), 0.0)).astype(BF16)
        acc = acc + _dot(h, w2_ref[lo:hi, :])
    out_ref[...] = _ln(ALPHA * x + acc, g_ref[...], b_ref[...])


def _mlp_kernel(x_ref, w1_ref, w2_ref, g_ref, b_ref, out_ref):
    _mlp_body(x_ref[...], w1_ref, w2_ref, g_ref, b_ref, out_ref)


def _attn_mlp_kernel(x_ref, o_ref, wo_ref, ga_ref, ba_ref, w1_ref, w2_ref, g_ref, b_ref, out_ref):
    x = _ln(ALPHA * x_ref[...] + _dot(o_ref[...], wo_ref[...]), ga_ref[...], ba_ref[...])
    _mlp_body(x, w1_ref, w2_ref, g_ref, b_ref, out_ref)


def _mlp_layer(x, w1, w2, g, b, *, tm, attn=None):
    m = x.shape[0]
    row_spec = pl.BlockSpec((tm, D_MODEL), lambda i: (i, 0))
    vec = _const_spec((1, D_MODEL))
    mlp_specs = [_const_spec((D_MODEL, D_FF)), _const_spec((D_FF, D_MODEL)), vec, vec]
    if attn is None:
        kern, args, specs = _mlp_kernel, (x, w1, w2, g, b), [row_spec] + mlp_specs
    else:
        o, wo, ga, ba = attn
        kern = _attn_mlp_kernel
        args = (x, o, wo, ga, ba, w1, w2, g, b)
        specs = [row_spec, row_spec, _const_spec((D_MODEL, D_MODEL)), vec, vec] + mlp_specs
    return pl.pallas_call(
        kern, grid=(m // tm,), in_specs=specs, out_specs=row_spec,
        out_shape=jax.ShapeDtypeStruct((m, D_MODEL), F32),
        compiler_params=_params(1), name="mlp_layer",
    )(*args)


def _matmul_kernel(x_ref, w_ref, o_ref):
    o_ref[...] = _dot(x_ref[...].astype(BF16), w_ref[...])


def _matmul(x, w, *, tm, tn):
    m, k = x.shape
    n = w.shape[1]
    return pl.pallas_call(
        _matmul_kernel, grid=(m // tm, n // tn),
        in_specs=[pl.BlockSpec((tm, k), lambda i, j: (i, 0)), pl.BlockSpec((k, tn), lambda i, j: (0, j))],
        out_specs=pl.BlockSpec((tm, tn), lambda i, j: (i, j)),
        out_shape=jax.ShapeDtypeStruct((m, n), F32),
        compiler_params=_params(2), name="qkv_projection",
    )(x, w)


def _attend_block(q2, ks, vs, bias, no_prev, head_a, in_prev):
    qb = (q2 * Q_SCALE).astype(BF16)
    zero = jnp.zeros_like(qb)
    qs = jnp.concatenate([jnp.where(head_a, qb, zero), jnp.where(head_a, zero, qb)], axis=0)
    s = _dot_nt(qs, ks.astype(BF16)) + bias
    s = s + jnp.where(in_prev, no_prev, 0.0)
    m = jnp.max(s, axis=-1, keepdims=True)
    p = jnp.exp(s - m)
    den = jnp.sum(p, axis=-1, keepdims=True)
    o = _dot(p.astype(BF16), vs.astype(BF16)) / den
    lse = m + jnp.log(den)
    o2 = jnp.where(head_a, o[:KEY_BLOCK], o[KEY_BLOCK:])
    l2 = jnp.where(head_a, lse[:KEY_BLOCK], lse[KEY_BLOCK:])
    return o2, l2


def _prompt_attn_kernel(*refs):
    ins, (bias_ref, o_ref), scr = refs[:15], refs[15:17], refs[17:]
    oscs, lscs = scr[0:3], scr[3:6]
    t = pl.program_id(2)
    lane = lax.broadcasted_iota(jnp.int32, (KEY_BLOCK, LANES), 1)
    head_a = lane < HEAD_DIM
    kcol = lax.broadcasted_iota(jnp.int32, (2 * KEY_BLOCK, 2 * KEY_BLOCK), 1)
    in_prev = kcol < KEY_BLOCK
    no_prev_first = jnp.where(t == 0, NEG, 0.0)

    for g, d in enumerate(DILATIONS):
        q_ref, kc_ref, vc_ref, kp_ref, vp_ref = ins[5 * g:5 * g + 5]
        osc, lsc = oscs[g], lscs[g]
        nqb = Q_TILE // d // KEY_BLOCK

        def rows(ref, start, size, d=d):
            if d == 1:
                return pl.ds(pl.multiple_of(start, KEY_BLOCK), size)
            return pl.ds(start, size, stride=d)

        def first_blocks(r, carry, d=d, g=g, q_ref=q_ref, kc_ref=kc_ref, vc_ref=vc_ref, kp_ref=kp_ref,
                         vp_ref=vp_ref, osc=osc, lsc=lsc, rows=rows):
            sel = rows(None, r, KEY_BLOCK)
            ks = jnp.concatenate([kp_ref[sel, :], kc_ref[sel, :]], axis=0)
            vs = jnp.concatenate([vp_ref[sel, :], vc_ref[sel, :]], axis=0)
            bias = bias_ref[g].reshape(2 * KEY_BLOCK, 2 * KEY_BLOCK)
            o2, l2 = _attend_block(q_ref[sel, :], ks, vs, bias, no_prev_first, head_a, in_prev)
            osc[sel, :] = o2
            lsc[sel, :] = l2
            return carry

        lax.fori_loop(0, d, first_blocks, 0)

        if nqb > 1:
            def later_blocks(sp, carry, d=d, g=g, nqb=nqb, q_ref=q_ref, kc_ref=kc_ref, vc_ref=vc_ref,
                             osc=osc, lsc=lsc, rows=rows):
                r = sp // (nqb - 1)
                qb = 1 + sp % (nqb - 1)
                q_sel = rows(None, d * KEY_BLOCK * qb + r, KEY_BLOCK)
                k_sel = rows(None, d * KEY_BLOCK * (qb - 1) + r, 2 * KEY_BLOCK)
                bias = bias_ref[g].reshape(2 * KEY_BLOCK, 2 * KEY_BLOCK)
                o2, l2 = _attend_block(q_ref[q_sel, :], kc_ref[k_sel, :], vc_ref[k_sel, :], bias, 0.0,
                                       head_a, in_prev)
                osc[q_sel, :] = o2
                lsc[q_sel, :] = l2
                return carry

            lax.fori_loop(0, d * (nqb - 1), later_blocks, 0)

    merge_rows = 256
    for c in range(Q_TILE // merge_rows):
        rs = slice(c * merge_rows, (c + 1) * merge_rows)
        ls = [lscs[g][rs, :] for g in range(N_BRANCH)]
        m = jnp.maximum(jnp.maximum(ls[0], ls[1]), ls[2])
        es = [jnp.exp(l - m) for l in ls]
        num = es[0] * oscs[0][rs, :] + es[1] * oscs[1][rs, :] + es[2] * oscs[2][rs, :]
        o_ref[rs, :] = (num / (es[0] + es[1] + es[2])).astype(o_ref.dtype)


def _prompt_attention(qkv, bias_tiles, *, batch, seq):
    n_tiles = seq // Q_TILE
    in_specs = []
    for g, d in enumerate(DILATIONS):
        prev_rows = KEY_BLOCK * d
        per_tile = Q_TILE // prev_rows
        q_col = g * N_PAIRS
        k_col = (KV_OFFSET + 2 * g * D_MODEL) // LANES
        v_col = k_col + N_PAIRS

        def cur(col):
            return pl.BlockSpec((Q_TILE, LANES), lambda hp, b, t, col=col: (b * n_tiles + t, col + hp))

        def prev(col, per_tile=per_tile, prev_rows=prev_rows):
            return pl.BlockSpec(
                (prev_rows, LANES),
                lambda hp, b, t, col=col: (jnp.maximum((b * n_tiles + t) * per_tile - 1, 0), col + hp))

        in_specs += [cur(q_col), cur(k_col), cur(v_col), prev(k_col), prev(v_col)]
    in_specs.append(pl.BlockSpec((N_BRANCH, HEADS_PER_STEP, KEY_BLOCK, 2 * KEY_BLOCK),
                                 lambda hp, b, t: (0, hp, 0, 0)))
    return pl.pallas_call(
        _prompt_attn_kernel,
        grid=(N_PAIRS, batch, n_tiles),
        in_specs=in_specs,
        out_specs=pl.BlockSpec((Q_TILE, LANES), lambda hp, b, t: (b * n_tiles + t, hp)),
        out_shape=jax.ShapeDtypeStruct((batch * seq, D_MODEL), BF16),
        scratch_shapes=[pltpu.VMEM((Q_TILE, LANES), F32)] * (2 * N_BRANCH),
        compiler_params=_params(3), name="prompt_attention",
    )(*([qkv] * 15), bias_tiles)


def _sample_attn_kernel(qkv_ref, c0_ref, c1_ref, c2_ref, b0_ref, b1_ref, b2_ref, bn_ref, o_ref, *, n_new):
    caches = (c0_ref, c1_ref, c2_ref)
    cbias = (b0_ref, b1_ref, b2_ref)
    lane_head = lax.broadcasted_iota(jnp.int32, (N_HEADS, D_MODEL), 1) // HEAD_DIM
    own_lanes = lane_head == lax.broadcasted_iota(jnp.int32, (N_HEADS, D_MODEL), 0)
    pad = jnp.zeros((KEY_BLOCK - NEW_ROWS, D_MODEL), F32)

    def per_head_rows(x):
        return jnp.concatenate(
            [jnp.sum(jnp.where(own_lanes, x[i * N_HEADS:(i + 1) * N_HEADS], 0.0), axis=0, keepdims=True)
             for i in range(n_new)], axis=0)

    outs, lses = [], []
    for g in range(N_BRANCH):
        q = qkv_ref[0, :, g * D_MODEL:(g + 1) * D_MODEL] * Q_SCALE
        k_lo = KV_OFFSET + 2 * g * D_MODEL
        k_new = jnp.concatenate([qkv_ref[0, :, k_lo:k_lo + D_MODEL], pad], axis=0).astype(BF16)
        v_new = jnp.concatenate([qkv_ref[0, :, k_lo + D_MODEL:k_lo + 2 * D_MODEL], pad], axis=0).astype(BF16)
        qbd = jnp.concatenate([jnp.where(own_lanes, q[i:i + 1, :], 0.0) for i in range(n_new)],
                              axis=0).astype(BF16)
        kc = caches[g][0, :, :D_MODEL].astype(BF16)
        vc = caches[g][0, :, D_MODEL:].astype(BF16)
        s_c = _dot_nt(qbd, kc) + cbias[g][...]
        s_n = _dot_nt(qbd, k_new) + bn_ref[g]
        m = jnp.maximum(jnp.max(s_c, axis=-1, keepdims=True), jnp.max(s_n, axis=-1, keepdims=True))
        p_c = jnp.exp(s_c - m)
        p_n = jnp.exp(s_n - m)
        den = jnp.sum(p_c, axis=-1, keepdims=True) + jnp.sum(p_n, axis=-1, keepdims=True)
        o = (_dot(p_c.astype(BF16), vc) + _dot(p_n.astype(BF16), v_new)) / den
        lse = jnp.broadcast_to(m + jnp.log(den), o.shape)
        outs.append(per_head_rows(o))
        lses.append(per_head_rows(lse))
    m = jnp.maximum(jnp.maximum(lses[0], lses[1]), lses[2])
    es = [jnp.exp(l - m) for l in lses]
    num = es[0] * outs[0] + es[1] * outs[1] + es[2] * outs[2]
    o_ref[0] = num / (es[0] + es[1] + es[2])


def _sample_attention(qkv, caches, cache_bias, new_bias, *, n_new):
    nb = qkv.shape[0]
    return pl.pallas_call(
        functools.partial(_sample_attn_kernel, n_new=n_new),
        grid=(nb,),
        in_specs=[pl.BlockSpec((1, NEW_ROWS, QKV_WIDTH), lambda b: (b, 0, 0))]
        + [pl.BlockSpec((1,) + c.shape[1:], lambda b: (b, 0, 0)) for c in caches]
        + [_const_spec(t.shape) for t in cache_bias] + [_const_spec(new_bias.shape)],
        out_specs=pl.BlockSpec((1, n_new, D_MODEL), lambda b: (b, 0, 0)),
        out_shape=jax.ShapeDtypeStruct((nb, n_new, D_MODEL), F32),
        compiler_params=_params(1), name="sample_attention",
    )(qkv, *caches, *cache_bias, new_bias)


def _t5_bucket(dist):
    max_exact = NUM_BUCKETS // 2
    n = jnp.maximum(dist, 1).astype(F32)
    large = max_exact + (jnp.log(n / max_exact) / math.log(MAX_DISTANCE / max_exact)
                         * (NUM_BUCKETS - max_exact)).astype(jnp.int32)
    large = jnp.minimum(large, NUM_BUCKETS - 1)
    return jnp.where(dist < max_exact, dist, large)


def _branch_bias(rel_bias, g):
    n_keys = WINDOWS[g] // DILATIONS[g] + 1
    dist = jnp.arange(n_keys, dtype=jnp.int32) * DILATIONS[g]
    return rel_bias[_t5_bucket(dist)][:, g * N_HEADS:(g + 1) * N_HEADS].T.astype(F32)


def _masked_bias(bias_g, diff, g):
    d, w = DILATIONS[g], WINDOWS[g]
    valid = (diff >= 0) & (diff <= w) & (diff % d == 0)
    step = jnp.clip(diff // d, 0, w // d)
    return jnp.where(valid[None], bias_g[:, step], NEG)


def _prompt_bias_tiles(rel_bias):
    qi = jnp.arange(KEY_BLOCK, dtype=jnp.int32)[:, None]
    ki = jnp.arange(2 * KEY_BLOCK, dtype=jnp.int32)[None, :]
    tiles = []
    for g in range(N_BRANCH):
        diff = (qi + KEY_BLOCK - ki) * DILATIONS[g]
        tiles.append(_masked_bias(_branch_bias(rel_bias, g), diff, g))
    return jnp.stack(tiles)


def _sample_bias_tables(rel_bias, n_new, cache_pos):
    q_pos = PAST_LEN + jnp.arange(n_new, dtype=jnp.int32)
    new_pos = PAST_LEN + jnp.arange(KEY_BLOCK, dtype=jnp.int32)
    is_new = jnp.arange(KEY_BLOCK) < n_new
    rows = lambda tab: tab.transpose(1, 0, 2).reshape(n_new * N_HEADS, -1)
    cache_tabs, new_tabs = [], []
    for g in range(N_BRANCH):
        bias_g = _branch_bias(rel_bias, g)
        cache_tabs.append(rows(_masked_bias(bias_g, q_pos[:, None] - cache_pos[g][None, :], g)))
        tab = _masked_bias(bias_g, q_pos[:, None] - new_pos[None, :], g)
        new_tabs.append(rows(jnp.where(is_new[None, None, :], tab, NEG)))
    return cache_tabs, jnp.stack(new_tabs)


def kernel(x_prompt, x_sample, cache_kv_w128, cache_kv_w512, cache_kv_w2048, ln_g, ln_b, gm_w_in, gm_b_in,
           gm_ln_g, gm_ln_b, gm_w_s, gm_b_s, gm_w_out, gm_b_out, w_kv, attn_w_q, attn_w_o, rel_bias,
           mlp_w1, mlp_w2):
    batch, seq, _ = x_prompt.shape
    n_samp, n_new, _ = x_sample.shape
    ms = n_samp * n_new
    assert DEPTH == 2 and gm_w_in.shape[0] == 1 and attn_w_q.shape[0] == 1
    assert seq % Q_TILE == 0 and ms % CHUNK == 0 and CHUNK % n_new == 0 and n_new <= NEW_ROWS

    bf = lambda w: w.astype(BF16)
    vec = lambda v: v.reshape(1, -1)
    w_in, w_out = bf(gm_w_in[0]), bf(gm_w_out[0])
    w1, w2 = bf(mlp_w1), bf(mlp_w2)
    wqkv = jnp.concatenate([bf(attn_w_q[0]).reshape(D_MODEL, KV_OFFSET),
                            bf(w_kv).reshape(D_MODEL, 2 * KV_OFFSET)], axis=1)
    wo = bf(attn_w_o[0])
    gm_args = (vec(gm_b_in[0]), vec(gm_ln_g[0]), vec(gm_ln_b[0]))
    gm_tail = (w_out, vec(gm_b_out[0]), vec(ln_g[0, 0]), vec(ln_b[0, 0]))

    xp = x_prompt.reshape(batch * seq, D_MODEL)
    xs = x_sample.reshape(ms, D_MODEL)
    x1p, vn_p = _gmlp_layer(xp, w_in, *gm_args, gm_w_s[0], gm_b_s[0].T, *gm_tail, tm=256, rows_per_seq=seq)
    reps = CHUNK // n_new
    same_seq = (jnp.arange(CHUNK)[:, None] // n_new) == (jnp.arange(CHUNK)[None, :] // n_new)
    ws_s = jnp.where(same_seq[None], jnp.tile(gm_w_s[0][:, :n_new, :n_new], (1, reps, reps)), 0.0)
    bs_s = jnp.tile(gm_b_s[0][:, :n_new], (1, reps)).T
    x1s, vn_s = _gmlp_layer(xs, w_in, *gm_args, ws_s, bs_s, *gm_tail, tm=ms, rows_per_seq=None)

    x2p = _mlp_layer(x1p, w1[0], w2[0], vec(ln_g[0, 1]), vec(ln_b[0, 1]), tm=512)
    x2s = _mlp_layer(x1s, w1[0], w2[0], vec(ln_g[0, 1]), vec(ln_b[0, 1]), tm=ms)

    qkv_p = _matmul(x2p, wqkv, tm=1024, tn=1024)
    qkv_s = _matmul(x2s, wqkv, tm=ms, tn=1024)

    o_p = _prompt_attention(qkv_p, _prompt_bias_tiles(rel_bias), batch=batch, seq=seq)

    assert PAST_LEN % DILATIONS[2] == 0 and n_new <= DILATIONS[1]
    flat = lambda c: c.reshape(n_samp, -1, 2 * D_MODEL)
    caches = (flat(cache_kv_w128), flat(cache_kv_w512),
              flat(cache_kv_w2048.reshape(n_samp, -1, DILATIONS[2], 2 * D_MODEL)[:, :, :n_new]))
    pos2 = jnp.arange(PAST_LEN - WINDOWS[2], PAST_LEN, dtype=jnp.int32).reshape(-1, DILATIONS[2])[:, :n_new]
    cache_pos = (jnp.arange(PAST_LEN - WINDOWS[0], PAST_LEN, dtype=jnp.int32),
                 jnp.arange(PAST_LEN - WINDOWS[1], PAST_LEN, dtype=jnp.int32), pos2.reshape(-1))
    cache_bias, new_bias = _sample_bias_tables(rel_bias, n_new, cache_pos)
    qkv_s3 = jnp.pad(qkv_s.reshape(n_samp, n_new, QKV_WIDTH), ((0, 0), (0, NEW_ROWS - n_new), (0, 0)))
    o_s = _sample_attention(qkv_s3, caches, cache_bias, new_bias, n_new=n_new)
    o_s = o_s.reshape(ms, D_MODEL).astype(BF16)

    attn_ln = (vec(ln_g[1, 0]), vec(ln_b[1, 0]))
    y_p = _mlp_layer(x2p, w1[1], w2[1], vec(ln_g[1, 1]), vec(ln_b[1, 1]), tm=512, attn=(o_p, wo) + attn_ln)
    y_s = _mlp_layer(x2s, w1[1], w2[1], vec(ln_g[1, 1]), vec(ln_b[1, 1]), tm=ms, attn=(o_s, wo) + attn_ln)

    def kv_rows(qkv, n_seq, rows, g):
        lo = KV_OFFSET + 2 * g * D_MODEL
        tail = qkv.reshape(n_seq, -1, QKV_WIDTH)[:, -rows:, lo:lo + 2 * D_MODEL]
        return tail.reshape(n_seq, rows, 2, N_HEADS, HEAD_DIM)

    kv_p_out = [kv_rows(qkv_p, batch, min(WINDOWS[g], seq), g) for g in range(N_BRANCH)]
    kv_s_out = [kv_rows(qkv_s, n_samp, n_new, g) for g in range(N_BRANCH)]
    return (y_p.reshape(batch, seq, D_MODEL), y_s.reshape(n_samp, n_new, D_MODEL),
            vn_p[None], vn_s.reshape(1, n_samp, n_new, GATE),
            kv_p_out[0], kv_s_out[0], kv_p_out[1], kv_s_out[1], kv_p_out[2], kv_s_out[2])
```

```python
import functools
import math

import jax
import jax.numpy as jnp
from jax import lax
from jax.experimental import pallas as pl
from jax.experimental.pallas import tpu as pltpu

D_MODEL = 1024
CHUNK = 128
GATE = 2 * D_MODEL
SGU_GROUPS = 8
SGU_GROUP_DIM = GATE // SGU_GROUPS
WINDOWS = (128, 512, 2048)
DILATIONS = (1, 4, 16)
N_BRANCH = 3
HEAD_DIM = 64
N_HEADS = 16
D_FF = 4 * D_MODEL
NUM_BUCKETS = 32
MAX_DISTANCE = 2048
PAST_LEN = 2048
DEPTH = 2
ALPHA = (2 * DEPTH) ** 0.25
LN_EPS = 1e-5
NEG = -1e30
Q_SCALE = HEAD_DIM ** -0.5

BF16 = jnp.bfloat16
F32 = jnp.float32

VMEM_LIMIT_BYTES = 56 * 1024 * 1024
LANES = 128
Q_TILE = 2048
KEY_BLOCK = 128
HEADS_PER_STEP = LANES // HEAD_DIM
N_PAIRS = N_HEADS // HEADS_PER_STEP
NEW_ROWS = 8
QKV_WIDTH = N_BRANCH * 3 * D_MODEL
KV_OFFSET = N_BRANCH * D_MODEL


def _params(n_axes):
    return pltpu.CompilerParams(dimension_semantics=("arbitrary",) * n_axes,
                                vmem_limit_bytes=VMEM_LIMIT_BYTES)


def _const_spec(shape):
    return pl.BlockSpec(shape, lambda *_: (0,) * len(shape), pipeline_mode=pl.Buffered(1))


def _ln(x, g, b):
    mu = jnp.mean(x, axis=-1, keepdims=True)
    xc = x - mu
    var = jnp.mean(xc * xc, axis=-1, keepdims=True)
    return xc * lax.rsqrt(var + LN_EPS) * g + b


def _dot(a, b):
    return jnp.dot(a, b, preferred_element_type=F32)


def _dot_nt(a, b):
    return lax.dot_general(a, b, (((1,), (1,)), ((), ())), preferred_element_type=F32)


def _gmlp_kernel(x_ref, w_in_ref, b_in_ref, gv_ref, bv_ref, ws_ref, bs_ref, w_out_ref, b_out_ref,
                 g_ref, b_ref, x1_ref, vn_ref, vnb_sc, gact_sc, *, tm, blocks_per_seq):
    x = x_ref[...]
    xb = x.astype(BF16)
    zv = _dot(xb, w_in_ref[:, GATE:]) + b_in_ref[:, GATE:]
    vn = _ln(jax.nn.gelu(zv), gv_ref[...], bv_ref[...])
    vnb_sc[...] = vn.astype(BF16)

    if blocks_per_seq is None:
        vn_ref[...] = vn
    else:
        @pl.when(pl.program_id(0) % blocks_per_seq == blocks_per_seq - 1)
        def _():
            vn_ref[0] = vn[tm - CHUNK:, :]

    row = lax.broadcasted_iota(jnp.int32, (CHUNK, CHUNK), 0)
    col = lax.broadcasted_iota(jnp.int32, (CHUNK, CHUNK), 1)
    causal = row >= col
    for g in range(SGU_GROUPS):
        lo, hi = g * SGU_GROUP_DIM, (g + 1) * SGU_GROUP_DIM
        zu = jax.nn.gelu(_dot(xb, w_in_ref[:, lo:hi]) + b_in_ref[:, lo:hi])
        wg = jnp.where(causal, ws_ref[g], 0.0).astype(BF16)
        bsg = bs_ref[:, g:g + 1]
        for c in range(tm // CHUNK):
            r0, r1 = c * CHUNK, (c + 1) * CHUNK
            s = _dot(wg, vnb_sc[r0:r1, lo:hi]) + bsg
            gact_sc[r0:r1, lo:hi] = (zu[r0:r1] * s).astype(BF16)
    y = _dot(gact_sc[...], w_out_ref[...]) + b_out_ref[...]
    x1_ref[...] = _ln(ALPHA * x + y, g_ref[...], b_ref[...])


def _gmlp_layer(x, w_in, b_in, gv, bv, ws, bs_t, w_out, b_out, g, b, *, tm, rows_per_seq):
    m = x.shape[0]
    if rows_per_seq is None:
        blocks_per_seq = None
        vn_shape = jax.ShapeDtypeStruct((m, GATE), F32)
        vn_spec = pl.BlockSpec((tm, GATE), lambda i: (i, 0))
    else:
        blocks_per_seq = rows_per_seq // tm
        vn_shape = jax.ShapeDtypeStruct((m // rows_per_seq, CHUNK, GATE), F32)
        vn_spec = pl.BlockSpec((1, CHUNK, GATE), lambda i: (i // blocks_per_seq, 0, 0))
    return pl.pallas_call(
        functools.partial(_gmlp_kernel, tm=tm, blocks_per_seq=blocks_per_seq),
        grid=(m // tm,),
        in_specs=[
            pl.BlockSpec((tm, D_MODEL), lambda i: (i, 0)),
            _const_spec((D_MODEL, 2 * GATE)), _const_spec((1, 2 * GATE)),
            _const_spec((1, GATE)), _const_spec((1, GATE)),
            _const_spec((SGU_GROUPS, CHUNK, CHUNK)), _const_spec((CHUNK, SGU_GROUPS)),
            _const_spec((GATE, D_MODEL)), _const_spec((1, D_MODEL)),
            _const_spec((1, D_MODEL)), _const_spec((1, D_MODEL)),
        ],
        out_specs=[pl.BlockSpec((tm, D_MODEL), lambda i: (i, 0)), vn_spec],
        out_shape=[jax.ShapeDtypeStruct((m, D_MODEL), F32), vn_shape],
        scratch_shapes=[pltpu.VMEM((tm, GATE), BF16), pltpu.VMEM((tm, GATE), BF16)],
        compiler_params=_params(1),
        name="gmlp_layer",
    )(x, w_in, b_in, gv, bv, ws, bs_t, w_out, b_out, g, b)


def _mlp_body(x, w1_ref, w2_ref, g_ref, b_ref, out_ref):
    xb = x.astype(BF16)
    acc = jnp.zeros_like(x)
    for j in range(D_FF // D_MODEL):
        lo, hi = j * D_MODEL, (j + 1) * D_MODEL
        h = jnp.square(jnp.maximum(_dot(xb, w1_ref[:, lo:hi]), 0.0)).astype(BF16)
        acc = acc + _dot(h, w2_ref[lo:hi, :])
    out_ref[...] = _ln(ALPHA * x + acc, g_ref[...], b_ref[...])


def _mlp_kernel(x_ref, w1_ref, w2_ref, g_ref, b_ref, out_ref):
    _mlp_body(x_ref[...], w1_ref, w2_ref, g_ref, b_ref, out_ref)


def _attn_mlp_kernel(x_ref, o_ref, wo_ref, ga_ref, ba_ref, w1_ref, w2_ref, g_ref, b_ref, out_ref):
    x = _ln(ALPHA * x_ref[...] + _dot(o_ref[...], wo_ref[...]), ga_ref[...], ba_ref[...])
    _mlp_body(x, w1_ref, w2_ref, g_ref, b_ref, out_ref)


def _mlp_layer(x, w1, w2, g, b, *, tm, attn=None):
    m = x.shape[0]
    row_spec = pl.BlockSpec((tm, D_MODEL), lambda i: (i, 0))
    vec = _const_spec((1, D_MODEL))
    mlp_specs = [_const_spec((D_MODEL, D_FF)), _const_spec((D_FF, D_MODEL)), vec, vec]
    if attn is None:
        kern, args, specs = _mlp_kernel, (x, w1, w2, g, b), [row_spec] + mlp_specs
    else:
        o, wo, ga, ba = attn
        kern = _attn_mlp_kernel
        args = (x, o, wo, ga, ba, w1, w2, g, b)
        specs = [row_spec, row_spec, _const_spec((D_MODEL, D_MODEL)), vec, vec] + mlp_specs
    return pl.pallas_call(
        kern, grid=(m // tm,), in_specs=specs, out_specs=row_spec,
        out_shape=jax.ShapeDtypeStruct((m, D_MODEL), F32),
        compiler_params=_params(1), name="mlp_layer",
    )(*args)


def _matmul_kernel(x_ref, w_ref, o_ref):
    o_ref[...] = _dot(x_ref[...].astype(BF16), w_ref[...])


def _matmul(x, w, *, tm, tn):
    m, k = x.shape
    n = w.shape[1]
    return pl.pallas_call(
        _matmul_kernel, grid=(m // tm, n // tn),
        in_specs=[pl.BlockSpec((tm, k), lambda i, j: (i, 0)), pl.BlockSpec((k, tn), lambda i, j: (0, j))],
        out_specs=pl.BlockSpec((tm, tn), lambda i, j: (i, j)),
        out_shape=jax.ShapeDtypeStruct((m, n), F32),
        compiler_params=_params(2), name="qkv_projection",
    )(x, w)


def _attend_block(q2, ks, vs, bias, no_prev, head_a, in_prev):
    qb = (q2 * Q_SCALE).astype(BF16)
    zero = jnp.zeros_like(qb)
    qs = jnp.concatenate([jnp.where(head_a, qb, zero), jnp.where(head_a, zero, qb)], axis=0)
    s = _dot_nt(qs, ks.astype(BF16)) + bias
    if no_prev is not None:
        s = s + jnp.where(in_prev, no_prev, 0.0)
    m = jnp.max(s, axis=-1, keepdims=True)
    p = jnp.exp(s - m)
    den = jnp.sum(p, axis=-1, keepdims=True)
    o = _dot(p.astype(BF16), vs.astype(BF16)) / den
    lse = m + jnp.log(den)
    o2 = jnp.where(head_a, o[:KEY_BLOCK], o[KEY_BLOCK:])
    l2 = jnp.where(head_a, lse[:KEY_BLOCK], lse[KEY_BLOCK:])
    return o2, l2


def _prompt_attn_kernel(*refs, unroll):
    ins, (steps_ref, step0_ref, o_ref), scr = refs[:15], refs[15:18], refs[18:]
    oscs, lscs, bias_sc = scr[0:3], scr[3:6], scr[6]
    t = pl.program_id(2)
    lane = lax.broadcasted_iota(jnp.int32, (KEY_BLOCK, LANES), 1)
    qrow = lax.broadcasted_iota(jnp.int32, (KEY_BLOCK, LANES), 0)
    head_a = lane < HEAD_DIM
    kcol = lax.broadcasted_iota(jnp.int32, (2 * KEY_BLOCK, 2 * KEY_BLOCK), 1)
    in_prev = kcol < KEY_BLOCK
    no_prev_first = jnp.where(t == 0, NEG, 0.0)

    for g in range(N_BRANCH):
        for hd in range(HEADS_PER_STEP):
            steps = jnp.broadcast_to(steps_ref[g, hd], (KEY_BLOCK, LANES))
            rolled = pltpu.roll(steps, 0, 1, stride=1, stride_axis=0)
            step0 = jnp.broadcast_to(step0_ref[g, hd], (KEY_BLOCK, LANES))
            left = jnp.where(lane >= qrow, rolled, NEG)
            right = jnp.where(lane < qrow, rolled, jnp.where(lane == qrow, step0, NEG))
            bias_sc[g, hd * KEY_BLOCK:(hd + 1) * KEY_BLOCK, :] = jnp.concatenate([left, right], axis=1)

    for g, d in enumerate(DILATIONS):
        q_ref, kc_ref, vc_ref, kp_ref, vp_ref = ins[5 * g:5 * g + 5]
        osc, lsc = oscs[g], lscs[g]
        nqb = Q_TILE // d // KEY_BLOCK

        def rows(start, size, d=d):
            if d == 1:
                return pl.ds(pl.multiple_of(start, KEY_BLOCK), size)
            return pl.ds(start, size, stride=d)

        def run(n_blocks, load, no_prev, g=g, osc=osc, lsc=lsc):
            group = max(u for u in range(1, unroll + 1) if n_blocks % u == 0)

            def body(it, carry):
                loaded = [load(it * group + u) for u in range(group)]
                bias = bias_sc[g]
                done = [(sel, _attend_block(q2, ks, vs, bias, no_prev, head_a, in_prev))
                        for sel, q2, ks, vs in loaded]
                for sel, (o2, l2) in done:
                    osc[sel, :] = o2
                    lsc[sel, :] = l2
                return carry

            lax.fori_loop(0, n_blocks // group, body, 0)

        def load_first(r, q_ref=q_ref, kc_ref=kc_ref, vc_ref=vc_ref, kp_ref=kp_ref, vp_ref=vp_ref, rows=rows):
            sel = rows(r, KEY_BLOCK)
            ks = jnp.concatenate([kp_ref[sel, :], kc_ref[sel, :]], axis=0)
            vs = jnp.concatenate([vp_ref[sel, :], vc_ref[sel, :]], axis=0)
            return sel, q_ref[sel, :], ks, vs

        def load_later(sp, d=d, nqb=nqb, q_ref=q_ref, kc_ref=kc_ref, vc_ref=vc_ref, rows=rows):
            r = sp // (nqb - 1)
            qb = 1 + sp % (nqb - 1)
            q_sel = rows(d * KEY_BLOCK * qb + r, KEY_BLOCK)
            k_sel = rows(d * KEY_BLOCK * (qb - 1) + r, 2 * KEY_BLOCK)
            return q_sel, q_ref[q_sel, :], kc_ref[k_sel, :], vc_ref[k_sel, :]

        run(d, load_first, no_prev_first)
        if nqb > 1:
            run(d * (nqb - 1), load_later, None)

    merge_rows = 256
    for c in range(Q_TILE // merge_rows):
        rs = slice(c * merge_rows, (c + 1) * merge_rows)
        ls = [lscs[g][rs, :] for g in range(N_BRANCH)]
        m = jnp.maximum(jnp.maximum(ls[0], ls[1]), ls[2])
        es = [jnp.exp(l - m) for l in ls]
        num = es[0] * oscs[0][rs, :] + es[1] * oscs[1][rs, :] + es[2] * oscs[2][rs, :]
        o_ref[rs, :] = (num / (es[0] + es[1] + es[2])).astype(o_ref.dtype)


def _prompt_attention(qkv, steps_rev, step0, *, batch, seq, unroll=4):
    n_tiles = seq // Q_TILE
    in_specs = []
    for g, d in enumerate(DILATIONS):
        prev_rows = KEY_BLOCK * d
        per_tile = Q_TILE // prev_rows
        q_col = g * N_PAIRS
        k_col = (KV_OFFSET + 2 * g * D_MODEL) // LANES
        v_col = k_col + N_PAIRS

        def cur(col):
            return pl.BlockSpec((Q_TILE, LANES), lambda hp, b, t, col=col: (b * n_tiles + t, col + hp))

        def prev(col, per_tile=per_tile, prev_rows=prev_rows):
            return pl.BlockSpec(
                (prev_rows, LANES),
                lambda hp, b, t, col=col: (jnp.maximum((b * n_tiles + t) * per_tile - 1, 0), col + hp))

        in_specs += [cur(q_col), cur(k_col), cur(v_col), prev(k_col), prev(v_col)]
    head_vec = pl.BlockSpec((N_BRANCH, HEADS_PER_STEP, 1, LANES), lambda hp, b, t: (0, hp, 0, 0))
    in_specs += [head_vec, head_vec]
    return pl.pallas_call(
        functools.partial(_prompt_attn_kernel, unroll=unroll),
        grid=(N_PAIRS, batch, n_tiles),
        in_specs=in_specs,
        out_specs=pl.BlockSpec((Q_TILE, LANES), lambda hp, b, t: (b * n_tiles + t, hp)),
        out_shape=jax.ShapeDtypeStruct((batch * seq, D_MODEL), BF16),
        scratch_shapes=[pltpu.VMEM((Q_TILE, LANES), F32)] * (2 * N_BRANCH)
        + [pltpu.VMEM((N_BRANCH, 2 * KEY_BLOCK, 2 * KEY_BLOCK), F32)],
        compiler_params=_params(3), name="prompt_attention",
    )(*([qkv] * 15), steps_rev, step0)


def _sample_attn_kernel(qkv_ref, c0_ref, c1_ref, c2_ref, b0_ref, b1_ref, b2_ref, bn_ref, o_ref, *, n_new, heads):
    caches = (c0_ref, c1_ref, c2_ref)
    cbias = (b0_ref, b1_ref, b2_ref)
    j = pl.program_id(1)
    width = heads * HEAD_DIM
    lane_head = lax.broadcasted_iota(jnp.int32, (heads, width), 1) // HEAD_DIM
    own_lanes = lane_head == lax.broadcasted_iota(jnp.int32, (heads, width), 0)

    def per_head_rows(x):
        return jnp.concatenate(
            [jnp.sum(jnp.where(own_lanes, x[i * heads:(i + 1) * heads], 0.0), axis=0, keepdims=True)
             for i in range(n_new)], axis=0)

    outs, lses = [], []
    for g in range(N_BRANCH):
        q = qkv_ref[0, g, 0] * Q_SCALE
        k_new = qkv_ref[0, g, 1].astype(BF16)
        v_new = qkv_ref[0, g, 2].astype(BF16)
        qbd = jnp.concatenate([jnp.where(own_lanes, q[i:i + 1, :], 0.0) for i in range(n_new)],
                              axis=0).astype(BF16)
        n_pos = caches[g].shape[-1]
        k_t = caches[g][0, 0].reshape(width, n_pos).astype(BF16)
        v_t = caches[g][0, 1].reshape(width, n_pos).astype(BF16)
        s_c = _dot(qbd, k_t) + cbias[g][j]
        s_n = _dot_nt(qbd, k_new) + bn_ref[g, j]
        m = jnp.maximum(jnp.max(s_c, axis=-1, keepdims=True), jnp.max(s_n, axis=-1, keepdims=True))
        p_c = jnp.exp(s_c - m)
        p_n = jnp.exp(s_n - m)
        den = jnp.sum(p_c, axis=-1, keepdims=True) + jnp.sum(p_n, axis=-1, keepdims=True)
        o = (_dot_nt(p_c.astype(BF16), v_t) + _dot(p_n.astype(BF16), v_new)) / den
        outs.append(per_head_rows(o))
        lses.append(per_head_rows(jnp.broadcast_to(m + jnp.log(den), o.shape)))
    m = jnp.maximum(jnp.maximum(lses[0], lses[1]), lses[2])
    es = [jnp.exp(l - m) for l in lses]
    num = es[0] * outs[0] + es[1] * outs[1] + es[2] * outs[2]
    merged = num / (es[0] + es[1] + es[2])
    o_ref[0] = jnp.concatenate([merged, jnp.zeros((NEW_ROWS - n_new, width), F32)], axis=0)


def _sample_attention(qkv, caches, cache_bias, new_bias, *, n_new, heads_per_step=8):
    nb = qkv.shape[0]
    hs = heads_per_step
    width = hs * HEAD_DIM
    return pl.pallas_call(
        functools.partial(_sample_attn_kernel, n_new=n_new, heads=hs),
        grid=(nb, N_HEADS // hs),
        in_specs=[pl.BlockSpec((1, N_BRANCH, 3, NEW_ROWS, width), lambda b, j: (b, 0, 0, 0, j))]
        + [pl.BlockSpec((1, 2, hs, HEAD_DIM, c.shape[-1]), lambda b, j: (b, 0, j, 0, 0)) for c in caches]
        + [_const_spec(t.shape) for t in cache_bias] + [_const_spec(new_bias.shape)],
        out_specs=pl.BlockSpec((1, NEW_ROWS, width), lambda b, j: (b, 0, j)),
        out_shape=jax.ShapeDtypeStruct((nb, NEW_ROWS, D_MODEL), F32),
        compiler_params=_params(2), name="sample_attention",
    )(qkv, *caches, *cache_bias, new_bias)


def _t5_bucket(dist):
    max_exact = NUM_BUCKETS // 2
    n = jnp.maximum(dist, 1).astype(F32)
    large = max_exact + (jnp.log(n / max_exact) / math.log(MAX_DISTANCE / max_exact)
                         * (NUM_BUCKETS - max_exact)).astype(jnp.int32)
    large = jnp.minimum(large, NUM_BUCKETS - 1)
    return jnp.where(dist < max_exact, dist, large)


def _branch_bias(rel_bias, g):
    n_keys = WINDOWS[g] // DILATIONS[g] + 1
    dist = jnp.arange(n_keys, dtype=jnp.int32) * DILATIONS[g]
    return rel_bias[_t5_bucket(dist)][:, g * N_HEADS:(g + 1) * N_HEADS].T.astype(F32)


def _sample_bias_tables(biases, n_new, hs):
    neg = lambda *shape: jnp.full(shape, NEG, F32)
    by_step = lambda t: t.reshape(N_HEADS // hs, hs, n_new, -1).transpose(0, 2, 1, 3).reshape(
        N_HEADS // hs, n_new * hs, -1)
    cache_tabs, new_tabs = [], []
    for g, d in enumerate(DILATIONS):
        w = WINDOWS[g]
        n_steps = w // d
        b = biases[g]
        rows, new_rows = [], []
        for i in range(n_new):
            first = n_steps + i // d
            hit = b[:, min(first, n_steps):i // d:-1]
            hit = jnp.concatenate([neg(N_HEADS, n_steps - hit.shape[1]), hit], axis=1)
            grid = jnp.where((jnp.arange(d) == i % d)[None, None, :], hit[:, :, None], NEG)
            rows.append(grid.reshape(N_HEADS, w))
            cols = [b[:, (i - j) // d] if (j <= i and (i - j) % d == 0 and j < n_new) else neg(N_HEADS)
                    for j in range(NEW_ROWS)]
            new_rows.append(jnp.stack(cols, axis=1))
        cache_tabs.append(by_step(jnp.stack(rows, axis=1)))
        new_tabs.append(by_step(jnp.stack(new_rows, axis=1)))
    return cache_tabs, jnp.stack(new_tabs)


def kernel(x_prompt, x_sample, cache_kv_w128, cache_kv_w512, cache_kv_w2048, ln_g, ln_b, gm_w_in, gm_b_in,
           gm_ln_g, gm_ln_b, gm_w_s, gm_b_s, gm_w_out, gm_b_out, w_kv, attn_w_q, attn_w_o, rel_bias,
           mlp_w1, mlp_w2):
    batch, seq, _ = x_prompt.shape
    n_samp, n_new, _ = x_sample.shape
    ms = n_samp * n_new
    assert DEPTH == 2 and gm_w_in.shape[0] == 1 and attn_w_q.shape[0] == 1
    assert seq % Q_TILE == 0 and ms % CHUNK == 0 and CHUNK % n_new == 0 and n_new <= NEW_ROWS
    assert all(PAST_LEN >= w and w % d == 0 and w // d == KEY_BLOCK for w, d in zip(WINDOWS, DILATIONS))

    bf = lambda w: w.astype(BF16)
    vec = lambda v: v.reshape(1, -1)
    w_in, w_out = bf(gm_w_in[0]), bf(gm_w_out[0])
    w1, w2 = bf(mlp_w1), bf(mlp_w2)
    wqkv = jnp.concatenate([bf(attn_w_q[0]).reshape(D_MODEL, KV_OFFSET),
                            bf(w_kv).reshape(D_MODEL, 2 * KV_OFFSET)], axis=1)
    wo = bf(attn_w_o[0])
    gm_args = (vec(gm_b_in[0]), vec(gm_ln_g[0]), vec(gm_ln_b[0]))
    gm_tail = (w_out, vec(gm_b_out[0]), vec(ln_g[0, 0]), vec(ln_b[0, 0]))

    xp = x_prompt.reshape(batch * seq, D_MODEL)
    xs = x_sample.reshape(ms, D_MODEL)
    x1p, vn_p = _gmlp_layer(xp, w_in, *gm_args, gm_w_s[0], gm_b_s[0].T, *gm_tail, tm=256, rows_per_seq=seq)
    reps = CHUNK // n_new
    same_seq = (jnp.arange(CHUNK)[:, None] // n_new) == (jnp.arange(CHUNK)[None, :] // n_new)
    ws_s = jnp.where(same_seq[None], jnp.tile(gm_w_s[0][:, :n_new, :n_new], (1, reps, reps)), 0.0)
    bs_s = jnp.tile(gm_b_s[0][:, :n_new], (1, reps)).T
    x1s, vn_s = _gmlp_layer(xs, w_in, *gm_args, ws_s, bs_s, *gm_tail, tm=ms, rows_per_seq=None)

    x2p = _mlp_layer(x1p, w1[0], w2[0], vec(ln_g[0, 1]), vec(ln_b[0, 1]), tm=512)
    x2s = _mlp_layer(x1s, w1[0], w2[0], vec(ln_g[0, 1]), vec(ln_b[0, 1]), tm=ms)

    qkv_p = _matmul(x2p, wqkv, tm=1024, tn=1024)
    qkv_s = _matmul(x2s, wqkv, tm=ms, tn=1024)

    biases = [_branch_bias(rel_bias, g) for g in range(N_BRANCH)]
    steps_rev = jnp.stack([b[:, :0:-1] for b in biases])[:, :, None, :]
    step0 = jnp.stack([jnp.broadcast_to(b[:, :1], (N_HEADS, LANES)) for b in biases])[:, :, None, :]
    o_p = _prompt_attention(qkv_p, steps_rev, step0, batch=batch, seq=seq)

    caches = [jnp.transpose(c, (0, 2, 3, 4, 1)) for c in (cache_kv_w128, cache_kv_w512, cache_kv_w2048)]
    heads_per_step = 8
    cache_bias, new_bias = _sample_bias_tables(biases, n_new, heads_per_step)
    q_s = qkv_s[:, :KV_OFFSET].reshape(n_samp, n_new, N_BRANCH, 1, D_MODEL)
    kv_s = qkv_s[:, KV_OFFSET:].reshape(n_samp, n_new, N_BRANCH, 2, D_MODEL)
    qkv_s5 = jnp.concatenate([q_s, kv_s], axis=3).transpose(0, 2, 3, 1, 4)
    qkv_s5 = jnp.pad(qkv_s5, ((0, 0),) * 3 + ((0, NEW_ROWS - n_new), (0, 0)))
    o_s = _sample_attention(qkv_s5, caches, cache_bias, new_bias, n_new=n_new, heads_per_step=heads_per_step)
    o_s = o_s[:, :n_new].reshape(ms, D_MODEL).astype(BF16)

    attn_ln = (vec(ln_g[1, 0]), vec(ln_b[1, 0]))
    y_p = _mlp_layer(x2p, w1[1], w2[1], vec(ln_g[1, 1]), vec(ln_b[1, 1]), tm=512, attn=(o_p, wo) + attn_ln)
    y_s = _mlp_layer(x2s, w1[1], w2[1], vec(ln_g[1, 1]), vec(ln_b[1, 1]), tm=ms, attn=(o_s, wo) + attn_ln)

    def kv_rows(qkv, n_seq, rows, g):
        lo = KV_OFFSET + 2 * g * D_MODEL
        tail = qkv.reshape(n_seq, -1, QKV_WIDTH)[:, -rows:, lo:lo + 2 * D_MODEL]
        return tail.reshape(n_seq, rows, 2, N_HEADS, HEAD_DIM)

    kv_p_out = [kv_rows(qkv_p, batch, min(WINDOWS[g], seq), g) for g in range(N_BRANCH)]
    kv_s_out = [kv_rows(qkv_s, n_samp, n_new, g) for g in range(N_BRANCH)]
    return (y_p.reshape(batch, seq, D_MODEL), y_s.reshape(n_samp, n_new, D_MODEL),
            vn_p[None], vn_s.reshape(1, n_samp, n_new, GATE),
            kv_p_out[0], kv_s_out[0], kv_p_out[1], kv_s_out[1], kv_p_out[2], kv_s_out[2])
```

```python
import functools
import math

import jax
import jax.numpy as jnp
from jax import lax
from jax.experimental import pallas as pl
from jax.experimental.pallas import tpu as pltpu

D_MODEL = 1024
CHUNK = 128
GATE = 2 * D_MODEL
SGU_GROUPS = 8
SGU_GROUP_DIM = GATE // SGU_GROUPS
WINDOWS = (128, 512, 2048)
DILATIONS = (1, 4, 16)
N_BRANCH = 3
HEAD_DIM = 64
N_HEADS = 16
D_FF = 4 * D_MODEL
NUM_BUCKETS = 32
MAX_DISTANCE = 2048
PAST_LEN = 2048
DEPTH = 2
ALPHA = (2 * DEPTH) ** 0.25
LN_EPS = 1e-5
NEG = -1e30
Q_SCALE = HEAD_DIM ** -0.5

BF16 = jnp.bfloat16
F32 = jnp.float32

VMEM_LIMIT_BYTES = 56 * 1024 * 1024
LANES = 128
Q_TILE = 2048
KEY_BLOCK = 128
HEADS_PER_STEP = LANES // HEAD_DIM
N_PAIRS = N_HEADS // HEADS_PER_STEP
NEW_ROWS = 8
QKV_WIDTH = N_BRANCH * 3 * D_MODEL
KV_OFFSET = N_BRANCH * D_MODEL


def _params(n_axes):
    return pltpu.CompilerParams(dimension_semantics=("arbitrary",) * n_axes,
                                vmem_limit_bytes=VMEM_LIMIT_BYTES)


def _const_spec(shape):
    return pl.BlockSpec(shape, lambda *_: (0,) * len(shape), pipeline_mode=pl.Buffered(1))


def _ln(x, g, b):
    mu = jnp.mean(x, axis=-1, keepdims=True)
    xc = x - mu
    var = jnp.mean(xc * xc, axis=-1, keepdims=True)
    return xc * lax.rsqrt(var + LN_EPS) * g + b


def _dot(a, b):
    return jnp.dot(a, b, preferred_element_type=F32)


def _dot_nt(a, b):
    return lax.dot_general(a, b, (((1,), (1,)), ((), ())), preferred_element_type=F32)


def _gmlp_kernel(x_ref, w_in_ref, b_in_ref, gv_ref, bv_ref, ws_ref, bs_ref, w_out_ref, b_out_ref,
                 g_ref, b_ref, x1_ref, vn_ref, vnb_sc, gact_sc, *, tm, blocks_per_seq):
    x = x_ref[...]
    xb = x.astype(BF16)
    zv = _dot(xb, w_in_ref[:, GATE:]) + b_in_ref[:, GATE:]
    vn = _ln(jax.nn.gelu(zv), gv_ref[...], bv_ref[...])
    vnb_sc[...] = vn.astype(BF16)

    if blocks_per_seq is None:
        vn_ref[...] = vn
    else:
        @pl.when(pl.program_id(0) % blocks_per_seq == blocks_per_seq - 1)
        def _():
            vn_ref[0] = vn[tm - CHUNK:, :]

    row = lax.broadcasted_iota(jnp.int32, (CHUNK, CHUNK), 0)
    col = lax.broadcasted_iota(jnp.int32, (CHUNK, CHUNK), 1)
    causal = row >= col
    for g in range(SGU_GROUPS):
        lo, hi = g * SGU_GROUP_DIM, (g + 1) * SGU_GROUP_DIM
        zu = jax.nn.gelu(_dot(xb, w_in_ref[:, lo:hi]) + b_in_ref[:, lo:hi])
        wg = jnp.where(causal, ws_ref[g], 0.0).astype(BF16)
        bsg = bs_ref[:, g:g + 1]
        for c in range(tm // CHUNK):
            r0, r1 = c * CHUNK, (c + 1) * CHUNK
            s = _dot(wg, vnb_sc[r0:r1, lo:hi]) + bsg
            gact_sc[r0:r1, lo:hi] = (zu[r0:r1] * s).astype(BF16)
    y = _dot(gact_sc[...], w_out_ref[...]) + b_out_ref[...]
    x1_ref[...] = _ln(ALPHA * x + y, g_ref[...], b_ref[...])


def _gmlp_layer(x, w_in, b_in, gv, bv, ws, bs_t, w_out, b_out, g, b, *, tm, rows_per_seq):
    m = x.shape[0]
    if rows_per_seq is None:
        blocks_per_seq = None
        vn_shape = jax.ShapeDtypeStruct((m, GATE), F32)
        vn_spec = pl.BlockSpec((tm, GATE), lambda i: (i, 0))
    else:
        blocks_per_seq = rows_per_seq // tm
        vn_shape = jax.ShapeDtypeStruct((m // rows_per_seq, CHUNK, GATE), F32)
        vn_spec = pl.BlockSpec((1, CHUNK, GATE), lambda i: (i // blocks_per_seq, 0, 0))
    return pl.pallas_call(
        functools.partial(_gmlp_kernel, tm=tm, blocks_per_seq=blocks_per_seq),
        grid=(m // tm,),
        in_specs=[
            pl.BlockSpec((tm, D_MODEL), lambda i: (i, 0)),
            _const_spec((D_MODEL, 2 * GATE)), _const_spec((1, 2 * GATE)),
            _const_spec((1, GATE)), _const_spec((1, GATE)),
            _const_spec((SGU_GROUPS, CHUNK, CHUNK)), _const_spec((CHUNK, SGU_GROUPS)),
            _const_spec((GATE, D_MODEL)), _const_spec((1, D_MODEL)),
            _const_spec((1, D_MODEL)), _const_spec((1, D_MODEL)),
        ],
        out_specs=[pl.BlockSpec((tm, D_MODEL), lambda i: (i, 0)), vn_spec],
        out_shape=[jax.ShapeDtypeStruct((m, D_MODEL), F32), vn_shape],
        scratch_shapes=[pltpu.VMEM((tm, GATE), BF16), pltpu.VMEM((tm, GATE), BF16)],
        compiler_params=_params(1),
        name="gmlp_layer",
    )(x, w_in, b_in, gv, bv, ws, bs_t, w_out, b_out, g, b)


def _mlp_body(x, w1_ref, w2_ref, g_ref, b_ref, out_ref):
    xb = x.astype(BF16)
    acc = jnp.zeros_like(x)
    for j in range(D_FF // D_MODEL):
        lo, hi = j * D_MODEL, (j + 1) * D_MODEL
        h = jnp.square(jnp.maximum(_dot(xb, w1_ref[:, lo:hi]), 0.0)).astype(BF16)
        acc = acc + _dot(h, w2_ref[lo:hi, :])
    out_ref[...] = _ln(ALPHA * x + acc, g_ref[...], b_ref[...])


def _mlp_kernel(x_ref, w1_ref, w2_ref, g_ref, b_ref, out_ref):
    _mlp_body(x_ref[...], w1_ref, w2_ref, g_ref, b_ref, out_ref)


def _attn_mlp_kernel(x_ref, o_ref, wo_ref, ga_ref, ba_ref, w1_ref, w2_ref, g_ref, b_ref, out_ref):
    x = _ln(ALPHA * x_ref[...] + _dot(o_ref[...], wo_ref[...]), ga_ref[...], ba_ref[...])
    _mlp_body(x, w1_ref, w2_ref, g_ref, b_ref, out_ref)


def _mlp_layer(x, w1, w2, g, b, *, tm, attn=None):
    m = x.shape[0]
    row_spec = pl.BlockSpec((tm, D_MODEL), lambda i: (i, 0))
    vec = _const_spec((1, D_MODEL))
    mlp_specs = [_const_spec((D_MODEL, D_FF)), _const_spec((D_FF, D_MODEL)), vec, vec]
    if attn is None:
        kern, args, specs = _mlp_kernel, (x, w1, w2, g, b), [row_spec] + mlp_specs
    else:
        o, wo, ga, ba = attn
        kern = _attn_mlp_kernel
        args = (x, o, wo, ga, ba, w1, w2, g, b)
        specs = [row_spec, row_spec, _const_spec((D_MODEL, D_MODEL)), vec, vec] + mlp_specs
    return pl.pallas_call(
        kern, grid=(m // tm,), in_specs=specs, out_specs=row_spec,
        out_shape=jax.ShapeDtypeStruct((m, D_MODEL), F32),
        compiler_params=_params(1), name="mlp_layer",
    )(*args)


def _matmul_kernel(x_ref, w_ref, o_ref, xb_sc):
    @pl.when(pl.program_id(1) == 0)
    def _():
        xb_sc[...] = x_ref[...].astype(BF16)

    o_ref[...] = _dot(xb_sc[...], w_ref[...])


def _matmul(x, w, *, tm, tn):
    m, k = x.shape
    n = w.shape[1]
    return pl.pallas_call(
        _matmul_kernel, grid=(m // tm, n // tn),
        in_specs=[pl.BlockSpec((tm, k), lambda i, j: (i, 0)), pl.BlockSpec((k, tn), lambda i, j: (0, j))],
        out_specs=pl.BlockSpec((tm, tn), lambda i, j: (i, j)),
        out_shape=jax.ShapeDtypeStruct((m, n), F32),
        scratch_shapes=[pltpu.VMEM((tm, k), BF16)],
        compiler_params=_params(2), name="qkv_projection",
    )(x, w)


def _attend_block(q2, ks, vs, bias, no_prev, head_a, in_prev):
    qb = (q2 * Q_SCALE).astype(BF16)
    zero = jnp.zeros_like(qb)
    qs = jnp.concatenate([jnp.where(head_a, qb, zero), jnp.where(head_a, zero, qb)], axis=0)
    s = _dot_nt(qs, ks.astype(BF16)) + bias
    if no_prev is not None:
        s = s + jnp.where(in_prev, no_prev, 0.0)
    m = jnp.max(s, axis=-1, keepdims=True)
    p = jnp.exp(s - m)
    vs_aug = jnp.concatenate([vs.astype(BF16), jnp.ones((2 * KEY_BLOCK, LANES), BF16)], axis=1)
    ov = _dot(p.astype(BF16), vs_aug)
    num2 = jnp.where(head_a, ov[:KEY_BLOCK, :LANES], ov[KEY_BLOCK:, :LANES])
    den2 = jnp.where(head_a, ov[:KEY_BLOCK, LANES:], ov[KEY_BLOCK:, LANES:])
    l2 = jnp.where(head_a, m[:KEY_BLOCK], m[KEY_BLOCK:]) + jnp.log(den2)
    return num2 / den2, l2


def _prompt_attn_kernel(*refs, max_group):
    ins, (steps_ref, step0_ref, o_ref), scr = refs[:15], refs[15:18], refs[18:]
    oscs, lscs, bias_sc = scr[0:3], scr[3:6], scr[6]
    t = pl.program_id(2)
    lane = lax.broadcasted_iota(jnp.int32, (KEY_BLOCK, LANES), 1)
    qrow = lax.broadcasted_iota(jnp.int32, (KEY_BLOCK, LANES), 0)
    head_a = lane < HEAD_DIM
    kcol = lax.broadcasted_iota(jnp.int32, (2 * KEY_BLOCK, 2 * KEY_BLOCK), 1)
    in_prev = kcol < KEY_BLOCK
    no_prev_first = jnp.where(t == 0, NEG, 0.0)

    for g in range(N_BRANCH):
        for hd in range(HEADS_PER_STEP):
            steps = jnp.broadcast_to(steps_ref[g, hd], (KEY_BLOCK, LANES))
            rolled = pltpu.roll(steps, 0, 1, stride=1, stride_axis=0)
            step0 = jnp.broadcast_to(step0_ref[g, hd], (KEY_BLOCK, LANES))
            left = jnp.where(lane >= qrow, rolled, NEG)
            right = jnp.where(lane < qrow, rolled, jnp.where(lane == qrow, step0, NEG))
            bias_sc[g, hd * KEY_BLOCK:(hd + 1) * KEY_BLOCK, :] = jnp.concatenate([left, right], axis=1)

    for g, d in enumerate(DILATIONS):
        q_ref, kc_ref, vc_ref, kp_ref, vp_ref = ins[5 * g:5 * g + 5]
        osc, lsc = oscs[g], lscs[g]
        nqb = Q_TILE // d // KEY_BLOCK

        def rows(start, size, d=d):
            if d == 1:
                return pl.ds(pl.multiple_of(start, KEY_BLOCK), size)
            return pl.ds(start, size, stride=d)

        def run(n_blocks, load, no_prev, g=g, osc=osc, lsc=lsc):
            group = max(u for u in range(1, max_group + 1) if n_blocks % u == 0)

            def body(it, carry):
                loaded = [load(it * group + u) for u in range(group)]
                bias = bias_sc[g]
                done = [(sel, _attend_block(q2, ks, vs, bias, no_prev, head_a, in_prev))
                        for sel, q2, ks, vs in loaded]
                for sel, (o2, l2) in done:
                    osc[sel, :] = o2
                    lsc[sel, :] = l2
                return carry

            lax.fori_loop(0, n_blocks // group, body, 0)

        def load_first(r, q_ref=q_ref, kc_ref=kc_ref, vc_ref=vc_ref, kp_ref=kp_ref, vp_ref=vp_ref, rows=rows):
            sel = rows(r, KEY_BLOCK)
            ks = jnp.concatenate([kp_ref[sel, :], kc_ref[sel, :]], axis=0)
            vs = jnp.concatenate([vp_ref[sel, :], vc_ref[sel, :]], axis=0)
            return sel, q_ref[sel, :], ks, vs

        def load_later(sp, d=d, nqb=nqb, q_ref=q_ref, kc_ref=kc_ref, vc_ref=vc_ref, rows=rows):
            r = sp // (nqb - 1)
            qb = 1 + sp % (nqb - 1)
            q_sel = rows(d * KEY_BLOCK * qb + r, KEY_BLOCK)
            k_sel = rows(d * KEY_BLOCK * (qb - 1) + r, 2 * KEY_BLOCK)
            return q_sel, q_ref[q_sel, :], kc_ref[k_sel, :], vc_ref[k_sel, :]

        run(d, load_first, no_prev_first)
        if nqb > 1:
            run(d * (nqb - 1), load_later, None)

    merge_rows = 256
    for c in range(Q_TILE // merge_rows):
        rs = slice(c * merge_rows, (c + 1) * merge_rows)
        ls = [lscs[g][rs, :] for g in range(N_BRANCH)]
        m = jnp.maximum(jnp.maximum(ls[0], ls[1]), ls[2])
        es = [jnp.exp(l - m) for l in ls]
        num = es[0] * oscs[0][rs, :] + es[1] * oscs[1][rs, :] + es[2] * oscs[2][rs, :]
        o_ref[rs, :] = (num / (es[0] + es[1] + es[2])).astype(o_ref.dtype)


def _prompt_attention(qkv, steps_rev, step0, *, batch, seq, max_group=6):
    n_tiles = seq // Q_TILE
    in_specs = []
    for g, d in enumerate(DILATIONS):
        prev_rows = KEY_BLOCK * d
        per_tile = Q_TILE // prev_rows
        q_col = g * N_PAIRS
        k_col = (KV_OFFSET + 2 * g * D_MODEL) // LANES
        v_col = k_col + N_PAIRS

        def cur(col):
            return pl.BlockSpec((Q_TILE, LANES), lambda hp, b, t, col=col: (b * n_tiles + t, col + hp))

        def prev(col, per_tile=per_tile, prev_rows=prev_rows):
            return pl.BlockSpec(
                (prev_rows, LANES),
                lambda hp, b, t, col=col: (jnp.maximum((b * n_tiles + t) * per_tile - 1, 0), col + hp))

        in_specs += [cur(q_col), cur(k_col), cur(v_col), prev(k_col), prev(v_col)]
    head_vec = pl.BlockSpec((N_BRANCH, HEADS_PER_STEP, 1, LANES), lambda hp, b, t: (0, hp, 0, 0))
    in_specs += [head_vec, head_vec]
    return pl.pallas_call(
        functools.partial(_prompt_attn_kernel, max_group=max_group),
        grid=(N_PAIRS, batch, n_tiles),
        in_specs=in_specs,
        out_specs=pl.BlockSpec((Q_TILE, LANES), lambda hp, b, t: (b * n_tiles + t, hp)),
        out_shape=jax.ShapeDtypeStruct((batch * seq, D_MODEL), BF16),
        scratch_shapes=[pltpu.VMEM((Q_TILE, LANES), F32)] * (2 * N_BRANCH)
        + [pltpu.VMEM((N_BRANCH, 2 * KEY_BLOCK, 2 * KEY_BLOCK), F32)],
        compiler_params=_params(3), name="prompt_attention",
    )(*([qkv] * 15), steps_rev, step0)


def _sample_attn_kernel(qkv_ref, c0_ref, c1_ref, c2_ref, b0_ref, b1_ref, b2_ref, bn_ref, o_ref, *, n_new, heads):
    caches = (c0_ref, c1_ref, c2_ref)
    cbias = (b0_ref, b1_ref, b2_ref)
    j = pl.program_id(1)
    width = heads * HEAD_DIM
    lane_head = lax.broadcasted_iota(jnp.int32, (heads, width), 1) // HEAD_DIM
    own_lanes = lane_head == lax.broadcasted_iota(jnp.int32, (heads, width), 0)

    def per_head_rows(x):
        return jnp.concatenate(
            [jnp.sum(jnp.where(own_lanes, x[i * heads:(i + 1) * heads], 0.0), axis=0, keepdims=True)
             for i in range(n_new)], axis=0)

    outs, lses = [], []
    for g in range(N_BRANCH):
        q = qkv_ref[0, g, 0] * Q_SCALE
        k_new = qkv_ref[0, g, 1].astype(BF16)
        v_new = qkv_ref[0, g, 2].astype(BF16)
        qbd = jnp.concatenate([jnp.where(own_lanes, q[i:i + 1, :], 0.0) for i in range(n_new)],
                              axis=0).astype(BF16)
        n_pos = caches[g].shape[-1]
        k_t = caches[g][0, 0].reshape(width, n_pos).astype(BF16)
        v_t = caches[g][0, 1].reshape(width, n_pos).astype(BF16)
        s_c = _dot(qbd, k_t) + cbias[g][j]
        s_n = _dot_nt(qbd, k_new) + bn_ref[g, j]
        m = jnp.maximum(jnp.max(s_c, axis=-1, keepdims=True), jnp.max(s_n, axis=-1, keepdims=True))
        p_c = jnp.exp(s_c - m)
        p_n = jnp.exp(s_n - m)
        den = jnp.sum(p_c, axis=-1, keepdims=True) + jnp.sum(p_n, axis=-1, keepdims=True)
        o = (_dot_nt(p_c.astype(BF16), v_t) + _dot(p_n.astype(BF16), v_new)) / den
        outs.append(per_head_rows(o))
        lses.append(per_head_rows(jnp.broadcast_to(m + jnp.log(den), o.shape)))
    m = jnp.maximum(jnp.maximum(lses[0], lses[1]), lses[2])
    es = [jnp.exp(l - m) for l in lses]
    num = es[0] * outs[0] + es[1] * outs[1] + es[2] * outs[2]
    merged = num / (es[0] + es[1] + es[2])
    o_ref[0] = jnp.concatenate([merged, jnp.zeros((NEW_ROWS - n_new, width), F32)], axis=0)


def _sample_attention(qkv, caches, cache_bias, new_bias, *, n_new, heads_per_step=8):
    nb = qkv.shape[0]
    hs = heads_per_step
    width = hs * HEAD_DIM
    return pl.pallas_call(
        functools.partial(_sample_attn_kernel, n_new=n_new, heads=hs),
        grid=(nb, N_HEADS // hs),
        in_specs=[pl.BlockSpec((1, N_BRANCH, 3, NEW_ROWS, width), lambda b, j: (b, 0, 0, 0, j))]
        + [pl.BlockSpec((1, 2, hs, HEAD_DIM, c.shape[-1]), lambda b, j: (b, 0, j, 0, 0)) for c in caches]
        + [_const_spec(t.shape) for t in cache_bias] + [_const_spec(new_bias.shape)],
        out_specs=pl.BlockSpec((1, NEW_ROWS, width), lambda b, j: (b, 0, j)),
        out_shape=jax.ShapeDtypeStruct((nb, NEW_ROWS, D_MODEL), F32),
        compiler_params=_params(2), name="sample_attention",
    )(qkv, *caches, *cache_bias, new_bias)


def _t5_bucket(dist):
    max_exact = NUM_BUCKETS // 2
    n = jnp.maximum(dist, 1).astype(F32)
    large = max_exact + (jnp.log(n / max_exact) / math.log(MAX_DISTANCE / max_exact)
                         * (NUM_BUCKETS - max_exact)).astype(jnp.int32)
    large = jnp.minimum(large, NUM_BUCKETS - 1)
    return jnp.where(dist < max_exact, dist, large)


def _branch_bias(rel_bias, g):
    n_keys = WINDOWS[g] // DILATIONS[g] + 1
    dist = jnp.arange(n_keys, dtype=jnp.int32) * DILATIONS[g]
    return rel_bias[_t5_bucket(dist)][:, g * N_HEADS:(g + 1) * N_HEADS].T.astype(F32)


def _sample_bias_tables(biases, n_new, hs):
    neg = lambda *shape: jnp.full(shape, NEG, F32)
    by_step = lambda t: t.reshape(N_HEADS // hs, hs, n_new, -1).transpose(0, 2, 1, 3).reshape(
        N_HEADS // hs, n_new * hs, -1)
    cache_tabs, new_tabs = [], []
    for g, d in enumerate(DILATIONS):
        w = WINDOWS[g]
        n_steps = w // d
        b = biases[g]
        rows, new_rows = [], []
        for i in range(n_new):
            first = n_steps + i // d
            hit = b[:, min(first, n_steps):i // d:-1]
            hit = jnp.concatenate([neg(N_HEADS, n_steps - hit.shape[1]), hit], axis=1)
            grid = jnp.where((jnp.arange(d) == i % d)[None, None, :], hit[:, :, None], NEG)
            rows.append(grid.reshape(N_HEADS, w))
            cols = [b[:, (i - j) // d] if (j <= i and (i - j) % d == 0 and j < n_new) else neg(N_HEADS)
                    for j in range(NEW_ROWS)]
            new_rows.append(jnp.stack(cols, axis=1))
        cache_tabs.append(by_step(jnp.stack(rows, axis=1)))
        new_tabs.append(by_step(jnp.stack(new_rows, axis=1)))
    return cache_tabs, jnp.stack(new_tabs)


def kernel(x_prompt, x_sample, cache_kv_w128, cache_kv_w512, cache_kv_w2048, ln_g, ln_b, gm_w_in, gm_b_in,
           gm_ln_g, gm_ln_b, gm_w_s, gm_b_s, gm_w_out, gm_b_out, w_kv, attn_w_q, attn_w_o, rel_bias,
           mlp_w1, mlp_w2):
    batch, seq, _ = x_prompt.shape
    n_samp, n_new, _ = x_sample.shape
    ms = n_samp * n_new
    assert DEPTH == 2 and gm_w_in.shape[0] == 1 and attn_w_q.shape[0] == 1
    assert seq % Q_TILE == 0 and ms % CHUNK == 0 and CHUNK % n_new == 0 and n_new <= NEW_ROWS
    assert all(PAST_LEN >= w and w % d == 0 and w // d == KEY_BLOCK for w, d in zip(WINDOWS, DILATIONS))

    bf = lambda w: w.astype(BF16)
    vec = lambda v: v.reshape(1, -1)
    w_in, w_out = bf(gm_w_in[0]), bf(gm_w_out[0])
    w1, w2 = bf(mlp_w1), bf(mlp_w2)
    wqkv = jnp.concatenate([bf(attn_w_q[0]).reshape(D_MODEL, KV_OFFSET),
                            bf(w_kv).reshape(D_MODEL, 2 * KV_OFFSET)], axis=1)
    wo = bf(attn_w_o[0])
    gm_args = (vec(gm_b_in[0]), vec(gm_ln_g[0]), vec(gm_ln_b[0]))
    gm_tail = (w_out, vec(gm_b_out[0]), vec(ln_g[0, 0]), vec(ln_b[0, 0]))

    xp = x_prompt.reshape(batch * seq, D_MODEL)
    xs = x_sample.reshape(ms, D_MODEL)
    x1p, vn_p = _gmlp_layer(xp, w_in, *gm_args, gm_w_s[0], gm_b_s[0].T, *gm_tail, tm=512, rows_per_seq=seq)
    reps = CHUNK // n_new
    same_seq = (jnp.arange(CHUNK)[:, None] // n_new) == (jnp.arange(CHUNK)[None, :] // n_new)
    ws_s = jnp.where(same_seq[None], jnp.tile(gm_w_s[0][:, :n_new, :n_new], (1, reps, reps)), 0.0)
    bs_s = jnp.tile(gm_b_s[0][:, :n_new], (1, reps)).T
    x1s, vn_s = _gmlp_layer(xs, w_in, *gm_args, ws_s, bs_s, *gm_tail, tm=ms, rows_per_seq=None)

    x2p = _mlp_layer(x1p, w1[0], w2[0], vec(ln_g[0, 1]), vec(ln_b[0, 1]), tm=512)
    x2s = _mlp_layer(x1s, w1[0], w2[0], vec(ln_g[0, 1]), vec(ln_b[0, 1]), tm=ms)

    qkv_p = _matmul(x2p, wqkv, tm=1024, tn=1536)
    qkv_s = _matmul(x2s, wqkv, tm=ms, tn=1536)

    biases = [_branch_bias(rel_bias, g) for g in range(N_BRANCH)]
    steps_rev = jnp.stack([b[:, :0:-1] for b in biases])[:, :, None, :]
    step0 = jnp.stack([jnp.broadcast_to(b[:, :1], (N_HEADS, LANES)) for b in biases])[:, :, None, :]
    o_p = _prompt_attention(qkv_p, steps_rev, step0, batch=batch, seq=seq)

    caches = [jnp.transpose(c, (0, 2, 3, 4, 1)) for c in (cache_kv_w128, cache_kv_w512, cache_kv_w2048)]
    heads_per_step = 8
    cache_bias, new_bias = _sample_bias_tables(biases, n_new, heads_per_step)
    q_s = qkv_s[:, :KV_OFFSET].reshape(n_samp, n_new, N_BRANCH, 1, D_MODEL)
    kv_s = qkv_s[:, KV_OFFSET:].reshape(n_samp, n_new, N_BRANCH, 2, D_MODEL)
    qkv_s5 = jnp.concatenate([q_s, kv_s], axis=3).transpose(0, 2, 3, 1, 4)
    qkv_s5 = jnp.pad(qkv_s5, ((0, 0),) * 3 + ((0, NEW_ROWS - n_new), (0, 0)))
    o_s = _sample_attention(qkv_s5, caches, cache_bias, new_bias, n_new=n_new, heads_per_step=heads_per_step)
    o_s = o_s[:, :n_new].reshape(ms, D_MODEL).astype(BF16)

    attn_ln = (vec(ln_g[1, 0]), vec(ln_b[1, 0]))
    y_p = _mlp_layer(x2p, w1[1], w2[1], vec(ln_g[1, 1]), vec(ln_b[1, 1]), tm=512, attn=(o_p, wo) + attn_ln)
    y_s = _mlp_layer(x2s, w1[1], w2[1], vec(ln_g[1, 1]), vec(ln_b[1, 1]), tm=ms, attn=(o_s, wo) + attn_ln)

    def kv_rows(qkv, n_seq, rows, g):
        lo = KV_OFFSET + 2 * g * D_MODEL
        tail = qkv.reshape(n_seq, -1, QKV_WIDTH)[:, -rows:, lo:lo + 2 * D_MODEL]
        return tail.reshape(n_seq, rows, 2, N_HEADS, HEAD_DIM)

    kv_p_out = [kv_rows(qkv_p, batch, min(WINDOWS[g], seq), g) for g in range(N_BRANCH)]
    kv_s_out = [kv_rows(qkv_s, n_samp, n_new, g) for g in range(N_BRANCH)]
    return (y_p.reshape(batch, seq, D_MODEL), y_s.reshape(n_samp, n_new, D_MODEL),
            vn_p[None], vn_s.reshape(1, n_samp, n_new, GATE),
            kv_p_out[0], kv_s_out[0], kv_p_out[1], kv_s_out[1], kv_p_out[2], kv_s_out[2])
```

```python
import functools
import math

import jax
import jax.numpy as jnp
from jax import lax
from jax.experimental import pallas as pl
from jax.experimental.pallas import tpu as pltpu

D_MODEL = 1024
CHUNK = 128
GATE = 2 * D_MODEL
SGU_GROUPS = 8
SGU_GROUP_DIM = GATE // SGU_GROUPS
WINDOWS = (128, 512, 2048)
DILATIONS = (1, 4, 16)
N_BRANCH = 3
HEAD_DIM = 64
N_HEADS = 16
D_FF = 4 * D_MODEL
NUM_BUCKETS = 32
MAX_DISTANCE = 2048
PAST_LEN = 2048
DEPTH = 2
ALPHA = (2 * DEPTH) ** 0.25
LN_EPS = 1e-5
NEG = -1e30
Q_SCALE = HEAD_DIM ** -0.5

BF16 = jnp.bfloat16
F32 = jnp.float32

VMEM_LIMIT_BYTES = 56 * 1024 * 1024
LANES = 128
Q_TILE = 2048
KEY_BLOCK = 128
HEADS_PER_STEP = LANES // HEAD_DIM
N_PAIRS = N_HEADS // HEADS_PER_STEP
NEW_ROWS = 8
QKV_WIDTH = N_BRANCH * 3 * D_MODEL
KV_OFFSET = N_BRANCH * D_MODEL


def _params(n_axes):
    return pltpu.CompilerParams(dimension_semantics=("arbitrary",) * n_axes,
                                vmem_limit_bytes=VMEM_LIMIT_BYTES)


def _const_spec(shape):
    return pl.BlockSpec(shape, lambda *_: (0,) * len(shape), pipeline_mode=pl.Buffered(1))


def _ln(x, g, b):
    mu = jnp.mean(x, axis=-1, keepdims=True)
    xc = x - mu
    var = jnp.mean(xc * xc, axis=-1, keepdims=True)
    return xc * lax.rsqrt(var + LN_EPS) * g + b


def _dot(a, b):
    return jnp.dot(a, b, preferred_element_type=F32)


def _dot_nt(a, b):
    return lax.dot_general(a, b, (((1,), (1,)), ((), ())), preferred_element_type=F32)


def _gmlp_kernel(x_ref, w_in_ref, b_in_ref, gv_ref, bv_ref, ws_ref, bs_ref, w_out_ref, b_out_ref,
                 g_ref, b_ref, x1_ref, vn_ref, vnb_sc, gact_sc, *, tm, blocks_per_seq):
    x = x_ref[...]
    xb = x.astype(BF16)
    zv = _dot(xb, w_in_ref[:, GATE:]) + b_in_ref[:, GATE:]
    vn = _ln(jax.nn.gelu(zv), gv_ref[...], bv_ref[...])
    vnb_sc[...] = vn.astype(BF16)

    if blocks_per_seq is None:
        vn_ref[...] = vn
    else:
        @pl.when(pl.program_id(0) % blocks_per_seq == blocks_per_seq - 1)
        def _():
            vn_ref[0] = vn[tm - CHUNK:, :]

    row = lax.broadcasted_iota(jnp.int32, (CHUNK, CHUNK), 0)
    col = lax.broadcasted_iota(jnp.int32, (CHUNK, CHUNK), 1)
    causal = row >= col
    for g in range(SGU_GROUPS):
        lo, hi = g * SGU_GROUP_DIM, (g + 1) * SGU_GROUP_DIM
        zu = jax.nn.gelu(_dot(xb, w_in_ref[:, lo:hi]) + b_in_ref[:, lo:hi])
        wg = jnp.where(causal, ws_ref[g], 0.0).astype(BF16)
        bsg = bs_ref[:, g:g + 1]
        for c in range(tm // CHUNK):
            r0, r1 = c * CHUNK, (c + 1) * CHUNK
            s = _dot(wg, vnb_sc[r0:r1, lo:hi]) + bsg
            gact_sc[r0:r1, lo:hi] = (zu[r0:r1] * s).astype(BF16)
    y = _dot(gact_sc[...], w_out_ref[...]) + b_out_ref[...]
    x1_ref[...] = _ln(ALPHA * x + y, g_ref[...], b_ref[...])


def _gmlp_layer(x, w_in, b_in, gv, bv, ws, bs_t, w_out, b_out, g, b, *, tm, rows_per_seq):
    m = x.shape[0]
    if rows_per_seq is None:
        blocks_per_seq = None
        vn_shape = jax.ShapeDtypeStruct((m, GATE), F32)
        vn_spec = pl.BlockSpec((tm, GATE), lambda i: (i, 0))
    else:
        blocks_per_seq = rows_per_seq // tm
        vn_shape = jax.ShapeDtypeStruct((m // rows_per_seq, CHUNK, GATE), F32)
        vn_spec = pl.BlockSpec((1, CHUNK, GATE), lambda i: (i // blocks_per_seq, 0, 0))
    return pl.pallas_call(
        functools.partial(_gmlp_kernel, tm=tm, blocks_per_seq=blocks_per_seq),
        grid=(m // tm,),
        in_specs=[
            pl.BlockSpec((tm, D_MODEL), lambda i: (i, 0)),
            _const_spec((D_MODEL, 2 * GATE)), _const_spec((1, 2 * GATE)),
            _const_spec((1, GATE)), _const_spec((1, GATE)),
            _const_spec((SGU_GROUPS, CHUNK, CHUNK)), _const_spec((CHUNK, SGU_GROUPS)),
            _const_spec((GATE, D_MODEL)), _const_spec((1, D_MODEL)),
            _const_spec((1, D_MODEL)), _const_spec((1, D_MODEL)),
        ],
        out_specs=[pl.BlockSpec((tm, D_MODEL), lambda i: (i, 0)), vn_spec],
        out_shape=[jax.ShapeDtypeStruct((m, D_MODEL), F32), vn_shape],
        scratch_shapes=[pltpu.VMEM((tm, GATE), BF16), pltpu.VMEM((tm, GATE), BF16)],
        compiler_params=_params(1),
        name="gmlp_layer",
    )(x, w_in, b_in, gv, bv, ws, bs_t, w_out, b_out, g, b)


def _mlp_body(x, w1_ref, w2_ref, g_ref, b_ref, out_ref):
    xb = x.astype(BF16)
    acc = jnp.zeros_like(x)
    for j in range(D_FF // D_MODEL):
        lo, hi = j * D_MODEL, (j + 1) * D_MODEL
        h = jnp.square(jnp.maximum(_dot(xb, w1_ref[:, lo:hi]), 0.0)).astype(BF16)
        acc = acc + _dot(h, w2_ref[lo:hi, :])
    out_ref[...] = _ln(ALPHA * x + acc, g_ref[...], b_ref[...])


def _mlp_kernel(x_ref, w1_ref, w2_ref, g_ref, b_ref, out_ref):
    _mlp_body(x_ref[...], w1_ref, w2_ref, g_ref, b_ref, out_ref)


def _attn_mlp_kernel(x_ref, o_ref, wo_ref, ga_ref, ba_ref, w1_ref, w2_ref, g_ref, b_ref, out_ref):
    x = _ln(ALPHA * x_ref[...] + _dot(o_ref[...], wo_ref[...]), ga_ref[...], ba_ref[...])
    _mlp_body(x, w1_ref, w2_ref, g_ref, b_ref, out_ref)


def _mlp_layer(x, w1, w2, layer, g, b, *, tm, attn=None):
    m = x.shape[0]
    row_spec = pl.BlockSpec((tm, D_MODEL), lambda i: (i, 0))
    vec = _const_spec((1, D_MODEL))
    one_layer = lambda *shape: pl.BlockSpec((None,) + shape, lambda i: (layer, 0, 0), pipeline_mode=pl.Buffered(1))
    mlp_specs = [one_layer(D_MODEL, D_FF), one_layer(D_FF, D_MODEL), vec, vec]
    if attn is None:
        kern, args, specs = _mlp_kernel, (x, w1, w2, g, b), [row_spec] + mlp_specs
    else:
        o, wo, ga, ba = attn
        kern = _attn_mlp_kernel
        args = (x, o, wo, ga, ba, w1, w2, g, b)
        specs = [row_spec, row_spec, _const_spec((D_MODEL, D_MODEL)), vec, vec] + mlp_specs
    return pl.pallas_call(
        kern, grid=(m // tm,), in_specs=specs, out_specs=row_spec,
        out_shape=jax.ShapeDtypeStruct((m, D_MODEL), F32),
        compiler_params=_params(1), name="mlp_layer",
    )(*args)


def _matmul_kernel(x_ref, w_ref, o_ref, xb_sc):
    @pl.when(pl.program_id(1) == 0)
    def _():
        xb_sc[...] = x_ref[...].astype(BF16)

    o_ref[...] = _dot(xb_sc[...], w_ref[...])


def _matmul(x, w, *, tm, tn):
    m, k = x.shape
    n = w.shape[1]
    return pl.pallas_call(
        _matmul_kernel, grid=(m // tm, n // tn),
        in_specs=[pl.BlockSpec((tm, k), lambda i, j: (i, 0)), pl.BlockSpec((k, tn), lambda i, j: (0, j))],
        out_specs=pl.BlockSpec((tm, tn), lambda i, j: (i, j)),
        out_shape=jax.ShapeDtypeStruct((m, n), F32),
        scratch_shapes=[pltpu.VMEM((tm, k), BF16)],
        compiler_params=_params(2), name="qkv_projection",
    )(x, w)


def _kv_state_kernel(x_ref, wt_ref, o_ref, wb_sc):
    @pl.when(jnp.logical_and(pl.program_id(0) == 0, pl.program_id(1) == 0))
    def _():
        wb_sc[...] = wt_ref[0].astype(BF16)

    o_ref[0] = _dot_nt(wb_sc[...], x_ref[...].astype(BF16))


def _kv_state(x, wkv_t, g, *, batch, seq, rows, tw):
    per_seq = seq // tw
    first = (seq - rows) // tw
    return pl.pallas_call(
        _kv_state_kernel, grid=(batch, rows // tw),
        in_specs=[pl.BlockSpec((tw, D_MODEL), lambda b, j: (b * per_seq + first + j, 0)),
                  pl.BlockSpec((1, 2 * D_MODEL, D_MODEL), lambda b, j: (g, 0, 0), pipeline_mode=pl.Buffered(1))],
        out_specs=pl.BlockSpec((1, 2 * D_MODEL, tw), lambda b, j: (b, 0, j)),
        out_shape=jax.ShapeDtypeStruct((batch, 2 * D_MODEL, rows), F32),
        scratch_shapes=[pltpu.VMEM((2 * D_MODEL, D_MODEL), BF16)],
        compiler_params=_params(2), name="kv_state",
    )(x, wkv_t)


def _attend_block(q2, ks, vs, bias, no_prev, head_a, in_prev):
    qb = (q2 * Q_SCALE).astype(BF16)
    zero = jnp.zeros_like(qb)
    qs = jnp.concatenate([jnp.where(head_a, qb, zero), jnp.where(head_a, zero, qb)], axis=0)
    s = _dot_nt(qs, ks.astype(BF16)) + bias
    if no_prev is not None:
        s = s + jnp.where(in_prev, no_prev, 0.0)
    m = jnp.max(s, axis=-1, keepdims=True)
    p = jnp.exp(s - m)
    vs_aug = jnp.concatenate([vs.astype(BF16), jnp.ones((2 * KEY_BLOCK, LANES), BF16)], axis=1)
    ov = _dot(p.astype(BF16), vs_aug)
    num2 = jnp.where(head_a, ov[:KEY_BLOCK, :LANES], ov[KEY_BLOCK:, :LANES])
    den2 = jnp.where(head_a, ov[:KEY_BLOCK, LANES:], ov[KEY_BLOCK:, LANES:])
    l2 = jnp.where(head_a, m[:KEY_BLOCK], m[KEY_BLOCK:]) + jnp.log(den2)
    return num2 / den2, l2


def _prompt_attn_kernel(*refs, max_group):
    ins, (steps_ref, step0_ref, o_ref), scr = refs[:15], refs[15:18], refs[18:]
    oscs, lscs, bias_sc = scr[0:3], scr[3:6], scr[6]
    t = pl.program_id(2)
    lane = lax.broadcasted_iota(jnp.int32, (KEY_BLOCK, LANES), 1)
    qrow = lax.broadcasted_iota(jnp.int32, (KEY_BLOCK, LANES), 0)
    head_a = lane < HEAD_DIM
    kcol = lax.broadcasted_iota(jnp.int32, (2 * KEY_BLOCK, 2 * KEY_BLOCK), 1)
    in_prev = kcol < KEY_BLOCK
    no_prev_first = jnp.where(t == 0, NEG, 0.0)

    for g in range(N_BRANCH):
        for hd in range(HEADS_PER_STEP):
            steps = jnp.broadcast_to(steps_ref[g, hd], (KEY_BLOCK, LANES))
            rolled = pltpu.roll(steps, 0, 1, stride=1, stride_axis=0)
            step0 = jnp.broadcast_to(step0_ref[g, hd], (KEY_BLOCK, LANES))
            left = jnp.where(lane >= qrow, rolled, NEG)
            right = jnp.where(lane < qrow, rolled, jnp.where(lane == qrow, step0, NEG))
            bias_sc[g, hd * KEY_BLOCK:(hd + 1) * KEY_BLOCK, :] = jnp.concatenate([left, right], axis=1)

    pooled, loops = [], []
    for g, d in enumerate(DILATIONS):
        q_ref, kc_ref, vc_ref, kp_ref, vp_ref = ins[5 * g:5 * g + 5]
        nqb = Q_TILE // d // KEY_BLOCK

        def rows(start, size, d=d):
            if d == 1:
                return pl.ds(start if isinstance(start, int) else pl.multiple_of(start, KEY_BLOCK), size)
            return pl.ds(start, size, stride=d)

        def load_first(r, q_ref=q_ref, kc_ref=kc_ref, vc_ref=vc_ref, kp_ref=kp_ref, vp_ref=vp_ref, rows=rows):
            sel = rows(r, KEY_BLOCK)
            ks = jnp.concatenate([kp_ref[sel, :], kc_ref[sel, :]], axis=0)
            vs = jnp.concatenate([vp_ref[sel, :], vc_ref[sel, :]], axis=0)
            return sel, q_ref[sel, :], ks, vs

        def load_later(sp, d=d, nqb=nqb, q_ref=q_ref, kc_ref=kc_ref, vc_ref=vc_ref, rows=rows):
            r = sp // (nqb - 1)
            qb = 1 + sp % (nqb - 1)
            q_sel = rows(d * KEY_BLOCK * qb + r, KEY_BLOCK)
            k_sel = rows(d * KEY_BLOCK * (qb - 1) + r, 2 * KEY_BLOCK)
            return q_sel, q_ref[q_sel, :], kc_ref[k_sel, :], vc_ref[k_sel, :]

        if d < max_group:
            pooled += [(g, load_first, r, no_prev_first) for r in range(d)]
        else:
            loops.append((g, d, load_first, no_prev_first))
        if nqb > 1:
            loops.append((g, d * (nqb - 1), load_later, None))

    def attend_group(items):
        loaded = [(g, no_prev) + load(idx) for g, load, idx, no_prev in items]
        done = [(g, sel, _attend_block(q2, ks, vs, bias_sc[g], no_prev, head_a, in_prev))
                for g, no_prev, sel, q2, ks, vs in loaded]
        for g, sel, (o2, l2) in done:
            oscs[g][sel, :] = o2
            lscs[g][sel, :] = l2

    for i in range(0, len(pooled), max_group):
        attend_group(pooled[i:i + max_group])
    for g, n_blocks, load, no_prev in loops:
        group = max(u for u in range(1, max_group + 1) if n_blocks % u == 0)

        def body(it, carry, g=g, load=load, no_prev=no_prev, group=group):
            attend_group([(g, load, it * group + u, no_prev) for u in range(group)])
            return carry

        lax.fori_loop(0, n_blocks // group, body, 0)

    merge_rows = 256
    for c in range(Q_TILE // merge_rows):
        rs = slice(c * merge_rows, (c + 1) * merge_rows)
        ls = [lscs[g][rs, :] for g in range(N_BRANCH)]
        m = jnp.maximum(jnp.maximum(ls[0], ls[1]), ls[2])
        es = [jnp.exp(l - m) for l in ls]
        num = es[0] * oscs[0][rs, :] + es[1] * oscs[1][rs, :] + es[2] * oscs[2][rs, :]
        o_ref[rs, :] = (num / (es[0] + es[1] + es[2])).astype(o_ref.dtype)


def _prompt_attention(qkv, steps_rev, step0, *, batch, seq, max_group=6):
    n_tiles = seq // Q_TILE
    in_specs = []
    for g, d in enumerate(DILATIONS):
        prev_rows = KEY_BLOCK * d
        per_tile = Q_TILE // prev_rows
        q_col = g * N_PAIRS
        k_col = (KV_OFFSET + 2 * g * D_MODEL) // LANES
        v_col = k_col + N_PAIRS

        def cur(col):
            return pl.BlockSpec((Q_TILE, LANES), lambda hp, b, t, col=col: (b * n_tiles + t, col + hp))

        def prev(col, per_tile=per_tile, prev_rows=prev_rows):
            return pl.BlockSpec(
                (prev_rows, LANES),
                lambda hp, b, t, col=col: (jnp.maximum((b * n_tiles + t) * per_tile - 1, 0), col + hp))

        in_specs += [cur(q_col), cur(k_col), cur(v_col), prev(k_col), prev(v_col)]
    head_vec = pl.BlockSpec((N_BRANCH, HEADS_PER_STEP, 1, LANES), lambda hp, b, t: (0, hp, 0, 0))
    in_specs += [head_vec, head_vec]
    return pl.pallas_call(
        functools.partial(_prompt_attn_kernel, max_group=max_group),
        grid=(N_PAIRS, batch, n_tiles),
        in_specs=in_specs,
        out_specs=pl.BlockSpec((Q_TILE, LANES), lambda hp, b, t: (b * n_tiles + t, hp)),
        out_shape=jax.ShapeDtypeStruct((batch * seq, D_MODEL), BF16),
        scratch_shapes=[pltpu.VMEM((Q_TILE, LANES), F32)] * (2 * N_BRANCH)
        + [pltpu.VMEM((N_BRANCH, 2 * KEY_BLOCK, 2 * KEY_BLOCK), F32)],
        compiler_params=_params(3), name="prompt_attention",
    )(*([qkv] * 15), steps_rev, step0)


def _sample_attn_kernel(qkv_ref, c0_ref, c1_ref, c2_ref, b0_ref, b1_ref, b2_ref, bn_ref, o_ref, *, n_new, heads):
    caches = (c0_ref, c1_ref, c2_ref)
    cbias = (b0_ref, b1_ref, b2_ref)
    j = pl.program_id(1)
    width = heads * HEAD_DIM
    lane_head = lax.broadcasted_iota(jnp.int32, (heads, width), 1) // HEAD_DIM
    own_lanes = lane_head == lax.broadcasted_iota(jnp.int32, (heads, width), 0)

    def per_head_rows(x):
        return jnp.concatenate(
            [jnp.sum(jnp.where(own_lanes, x[i * heads:(i + 1) * heads], 0.0), axis=0, keepdims=True)
             for i in range(n_new)], axis=0)

    outs, lses = [], []
    for g in range(N_BRANCH):
        q = qkv_ref[0, g, 0] * Q_SCALE
        k_new = qkv_ref[0, g, 1].astype(BF16)
        v_new = qkv_ref[0, g, 2].astype(BF16)
        qbd = jnp.concatenate([jnp.where(own_lanes, q[i:i + 1, :], 0.0) for i in range(n_new)],
                              axis=0).astype(BF16)
        n_pos = caches[g].shape[-1]
        k_t = caches[g][0, 0].reshape(width, n_pos).astype(BF16)
        v_t = caches[g][0, 1].reshape(width, n_pos).astype(BF16)
        s_c = _dot(qbd, k_t) + cbias[g][j]
        s_n = _dot_nt(qbd, k_new) + bn_ref[g, j]
        m = jnp.maximum(jnp.max(s_c, axis=-1, keepdims=True), jnp.max(s_n, axis=-1, keepdims=True))
        p_c = jnp.exp(s_c - m)
        p_n = jnp.exp(s_n - m)
        den = jnp.sum(p_c, axis=-1, keepdims=True) + jnp.sum(p_n, axis=-1, keepdims=True)
        o = (_dot_nt(p_c.astype(BF16), v_t) + _dot(p_n.astype(BF16), v_new)) / den
        outs.append(per_head_rows(o))
        lses.append(per_head_rows(jnp.broadcast_to(m + jnp.log(den), o.shape)))
    m = jnp.maximum(jnp.maximum(lses[0], lses[1]), lses[2])
    es = [jnp.exp(l - m) for l in lses]
    num = es[0] * outs[0] + es[1] * outs[1] + es[2] * outs[2]
    merged = num / (es[0] + es[1] + es[2])
    o_ref[0] = jnp.concatenate([merged, jnp.zeros((NEW_ROWS - n_new, width), F32)], axis=0)


def _sample_attention(qkv, caches, cache_bias, new_bias, *, n_new, heads_per_step=8):
    nb = qkv.shape[0]
    hs = heads_per_step
    width = hs * HEAD_DIM
    return pl.pallas_call(
        functools.partial(_sample_attn_kernel, n_new=n_new, heads=hs),
        grid=(nb, N_HEADS // hs),
        in_specs=[pl.BlockSpec((1, N_BRANCH, 3, NEW_ROWS, width), lambda b, j: (b, 0, 0, 0, j))]
        + [pl.BlockSpec((1, 2, hs, HEAD_DIM, c.shape[-1]), lambda b, j: (b, 0, j, 0, 0)) for c in caches]
        + [_const_spec(t.shape) for t in cache_bias] + [_const_spec(new_bias.shape)],
        out_specs=pl.BlockSpec((1, NEW_ROWS, width), lambda b, j: (b, 0, j)),
        out_shape=jax.ShapeDtypeStruct((nb, NEW_ROWS, D_MODEL), F32),
        compiler_params=_params(2), name="sample_attention",
    )(qkv, *caches, *cache_bias, new_bias)


def _t5_bucket(dist):
    max_exact = NUM_BUCKETS // 2
    n = jnp.maximum(dist, 1).astype(F32)
    large = max_exact + (jnp.log(n / max_exact) / math.log(MAX_DISTANCE / max_exact)
                         * (NUM_BUCKETS - max_exact)).astype(jnp.int32)
    large = jnp.minimum(large, NUM_BUCKETS - 1)
    return jnp.where(dist < max_exact, dist, large)


def _branch_bias(rel_bias, g):
    n_keys = WINDOWS[g] // DILATIONS[g] + 1
    dist = jnp.arange(n_keys, dtype=jnp.int32) * DILATIONS[g]
    return rel_bias[_t5_bucket(dist)][:, g * N_HEADS:(g + 1) * N_HEADS].T.astype(F32)


def _sample_bias_tables(biases, n_new, hs):
    neg = lambda *shape: jnp.full(shape, NEG, F32)
    by_step = lambda t: t.reshape(N_HEADS // hs, hs, n_new, -1).transpose(0, 2, 1, 3).reshape(
        N_HEADS // hs, n_new * hs, -1)
    cache_tabs, new_tabs = [], []
    for g, d in enumerate(DILATIONS):
        w = WINDOWS[g]
        n_steps = w // d
        b = biases[g]
        rows, new_rows = [], []
        for i in range(n_new):
            first = n_steps + i // d
            hit = b[:, min(first, n_steps):i // d:-1]
            hit = jnp.concatenate([neg(N_HEADS, n_steps - hit.shape[1]), hit], axis=1)
            grid = jnp.where((jnp.arange(d) == i % d)[None, None, :], hit[:, :, None], NEG)
            rows.append(grid.reshape(N_HEADS, w))
            cols = [b[:, (i - j) // d] if (j <= i and (i - j) % d == 0 and j < n_new) else neg(N_HEADS)
                    for j in range(NEW_ROWS)]
            new_rows.append(jnp.stack(cols, axis=1))
        cache_tabs.append(by_step(jnp.stack(rows, axis=1)))
        new_tabs.append(by_step(jnp.stack(new_rows, axis=1)))
    return cache_tabs, jnp.stack(new_tabs)


def kernel(x_prompt, x_sample, cache_kv_w128, cache_kv_w512, cache_kv_w2048, ln_g, ln_b, gm_w_in, gm_b_in,
           gm_ln_g, gm_ln_b, gm_w_s, gm_b_s, gm_w_out, gm_b_out, w_kv, attn_w_q, attn_w_o, rel_bias,
           mlp_w1, mlp_w2):
    batch, seq, _ = x_prompt.shape
    n_samp, n_new, _ = x_sample.shape
    ms = n_samp * n_new
    assert DEPTH == 2 and gm_w_in.shape[0] == 1 and attn_w_q.shape[0] == 1
    assert seq % Q_TILE == 0 and ms % CHUNK == 0 and CHUNK % n_new == 0 and n_new <= NEW_ROWS
    assert all(PAST_LEN >= w and w % d == 0 and w // d == KEY_BLOCK for w, d in zip(WINDOWS, DILATIONS))

    bf = lambda w: w.astype(BF16)
    vec = lambda v: v.reshape(1, -1)
    w_in, w_out = bf(gm_w_in[0]), bf(gm_w_out[0])
    w1, w2 = bf(mlp_w1), bf(mlp_w2)
    wqkv = jnp.concatenate([bf(attn_w_q[0]).reshape(D_MODEL, KV_OFFSET),
                            bf(w_kv).reshape(D_MODEL, 2 * KV_OFFSET)], axis=1)
    wo = bf(attn_w_o[0])
    gm_args = (vec(gm_b_in[0]), vec(gm_ln_g[0]), vec(gm_ln_b[0]))
    gm_tail = (w_out, vec(gm_b_out[0]), vec(ln_g[0, 0]), vec(ln_b[0, 0]))

    xp = x_prompt.reshape(batch * seq, D_MODEL)
    xs = x_sample.reshape(ms, D_MODEL)
    x1p, vn_p = _gmlp_layer(xp, w_in, *gm_args, gm_w_s[0], gm_b_s[0].T, *gm_tail, tm=512, rows_per_seq=seq)
    reps = CHUNK // n_new
    same_seq = (jnp.arange(CHUNK)[:, None] // n_new) == (jnp.arange(CHUNK)[None, :] // n_new)
    ws_s = jnp.where(same_seq[None], jnp.tile(gm_w_s[0][:, :n_new, :n_new], (1, reps, reps)), 0.0)
    bs_s = jnp.tile(gm_b_s[0][:, :n_new], (1, reps)).T
    x1s, vn_s = _gmlp_layer(xs, w_in, *gm_args, ws_s, bs_s, *gm_tail, tm=ms, rows_per_seq=None)

    x2p = _mlp_layer(x1p, w1, w2, 0, vec(ln_g[0, 1]), vec(ln_b[0, 1]), tm=512)
    x2s = _mlp_layer(x1s, w1, w2, 0, vec(ln_g[0, 1]), vec(ln_b[0, 1]), tm=ms)

    qkv_p = _matmul(x2p, wqkv, tm=1024, tn=1536)
    qkv_s = _matmul(x2s, wqkv, tm=ms, tn=1536)

    biases = [_branch_bias(rel_bias, g) for g in range(N_BRANCH)]
    steps_rev = jnp.stack([b[:, :0:-1] for b in biases])[:, :, None, :]
    step0 = jnp.stack([jnp.broadcast_to(b[:, :1], (N_HEADS, LANES)) for b in biases])[:, :, None, :]
    o_p = _prompt_attention(qkv_p, steps_rev, step0, batch=batch, seq=seq)

    caches = [jnp.transpose(c, (0, 2, 3, 4, 1)) for c in (cache_kv_w128, cache_kv_w512, cache_kv_w2048)]
    heads_per_step = 8
    cache_bias, new_bias = _sample_bias_tables(biases, n_new, heads_per_step)
    q_s = qkv_s[:, :KV_OFFSET].reshape(n_samp, n_new, N_BRANCH, 1, D_MODEL)
    kv_s = qkv_s[:, KV_OFFSET:].reshape(n_samp, n_new, N_BRANCH, 2, D_MODEL)
    qkv_s5 = jnp.concatenate([q_s, kv_s], axis=3).transpose(0, 2, 3, 1, 4)
    qkv_s5 = jnp.pad(qkv_s5, ((0, 0),) * 3 + ((0, NEW_ROWS - n_new), (0, 0)))
    o_s = _sample_attention(qkv_s5, caches, cache_bias, new_bias, n_new=n_new, heads_per_step=heads_per_step)
    o_s = o_s[:, :n_new].reshape(ms, D_MODEL).astype(BF16)

    attn_ln = (vec(ln_g[1, 0]), vec(ln_b[1, 0]))
    y_p = _mlp_layer(x2p, w1, w2, 1, vec(ln_g[1, 1]), vec(ln_b[1, 1]), tm=512, attn=(o_p, wo) + attn_ln)
    y_s = _mlp_layer(x2s, w1, w2, 1, vec(ln_g[1, 1]), vec(ln_b[1, 1]), tm=ms, attn=(o_s, wo) + attn_ln)

    def kv_rows(qkv, n_seq, rows, g):
        lo = KV_OFFSET + 2 * g * D_MODEL
        tail = qkv.reshape(n_seq, -1, QKV_WIDTH)[:, -rows:, lo:lo + 2 * D_MODEL]
        return tail.reshape(n_seq, rows, 2, N_HEADS, HEAD_DIM)

    wkv_t = jnp.transpose(w_kv, (1, 2, 3, 4, 0)).reshape(N_BRANCH, 2 * D_MODEL, D_MODEL)
    kv_p_out = []
    for g in range(N_BRANCH):
        rows = min(WINDOWS[g], seq)
        kv_t = _kv_state(x2p, wkv_t, g, batch=batch, seq=seq, rows=rows, tw=min(rows, 512))
        kv_p_out.append(kv_t.reshape(batch, 2, N_HEADS, HEAD_DIM, rows).transpose(0, 4, 1, 2, 3))
    kv_s_out = [kv_rows(qkv_s, n_samp, n_new, g) for g in range(N_BRANCH)]
    return (y_p.reshape(batch, seq, D_MODEL), y_s.reshape(n_samp, n_new, D_MODEL),
            vn_p[None], vn_s.reshape(1, n_samp, n_new, GATE),
            kv_p_out[0], kv_s_out[0], kv_p_out[1], kv_s_out[1], kv_p_out[2], kv_s_out[2])
```

```python
import functools
import math

import jax
import jax.numpy as jnp
from jax import lax
from jax.experimental import pallas as pl
from jax.experimental.pallas import tpu as pltpu

D_MODEL = 1024
CHUNK = 128
GATE = 2 * D_MODEL
SGU_GROUPS = 8
SGU_GROUP_DIM = GATE // SGU_GROUPS
WINDOWS = (128, 512, 2048)
DILATIONS = (1, 4, 16)
N_BRANCH = 3
HEAD_DIM = 64
N_HEADS = 16
D_FF = 4 * D_MODEL
NUM_BUCKETS = 32
MAX_DISTANCE = 2048
PAST_LEN = 2048
DEPTH = 2
ALPHA = (2 * DEPTH) ** 0.25
LN_EPS = 1e-5
NEG = -1e30
Q_SCALE = HEAD_DIM ** -0.5

BF16 = jnp.bfloat16
F32 = jnp.float32

VMEM_LIMIT_BYTES = 56 * 1024 * 1024
LANES = 128
Q_TILE = 2048
KEY_BLOCK = 128
HEADS_PER_STEP = LANES // HEAD_DIM
N_PAIRS = N_HEADS // HEADS_PER_STEP
NEW_ROWS = 8
QKV_WIDTH = N_BRANCH * 3 * D_MODEL
KV_OFFSET = N_BRANCH * D_MODEL


def _params(n_axes):
    return pltpu.CompilerParams(dimension_semantics=("arbitrary",) * n_axes,
                                vmem_limit_bytes=VMEM_LIMIT_BYTES)


def _const_spec(shape):
    return pl.BlockSpec(shape, lambda *_: (0,) * len(shape), pipeline_mode=pl.Buffered(1))


def _ln(x, g, b):
    mu = jnp.mean(x, axis=-1, keepdims=True)
    xc = x - mu
    var = jnp.mean(xc * xc, axis=-1, keepdims=True)
    return xc * lax.rsqrt(var + LN_EPS) * g + b


def _dot(a, b):
    return jnp.dot(a, b, preferred_element_type=F32)


def _dot_nt(a, b):
    return lax.dot_general(a, b, (((1,), (1,)), ((), ())), preferred_element_type=F32)


def _gmlp_kernel(x_ref, w_in_ref, b_in_ref, gv_ref, bv_ref, ws_ref, bs_ref, w_out_ref, b_out_ref,
                 g_ref, b_ref, x1_ref, vn_ref, vnb_sc, gact_sc, *, tm, blocks_per_seq):
    x = x_ref[...]
    xb = x.astype(BF16)
    zv = _dot(xb, w_in_ref[:, GATE:]) + b_in_ref[:, GATE:]
    vn = _ln(jax.nn.gelu(zv), gv_ref[...], bv_ref[...])
    vnb_sc[...] = vn.astype(BF16)

    if blocks_per_seq is None:
        vn_ref[...] = vn
    else:
        @pl.when(pl.program_id(0) % blocks_per_seq == blocks_per_seq - 1)
        def _():
            vn_ref[0] = vn[tm - CHUNK:, :]

    row = lax.broadcasted_iota(jnp.int32, (CHUNK, CHUNK), 0)
    col = lax.broadcasted_iota(jnp.int32, (CHUNK, CHUNK), 1)
    causal = row >= col
    for g in range(SGU_GROUPS):
        lo, hi = g * SGU_GROUP_DIM, (g + 1) * SGU_GROUP_DIM
        zu = jax.nn.gelu(_dot(xb, w_in_ref[:, lo:hi]) + b_in_ref[:, lo:hi])
        wg = jnp.where(causal, ws_ref[g], 0.0).astype(BF16)
        bsg = bs_ref[:, g:g + 1]
        for c in range(tm // CHUNK):
            r0, r1 = c * CHUNK, (c + 1) * CHUNK
            s = _dot(wg, vnb_sc[r0:r1, lo:hi]) + bsg
            gact_sc[r0:r1, lo:hi] = (zu[r0:r1] * s).astype(BF16)
    y = _dot(gact_sc[...], w_out_ref[...]) + b_out_ref[...]
    x1_ref[...] = _ln(ALPHA * x + y, g_ref[...], b_ref[...])


def _gmlp_layer(x, w_in, b_in, gv, bv, ws, bs_t, w_out, b_out, g, b, *, tm, rows_per_seq):
    m = x.shape[0]
    if rows_per_seq is None:
        blocks_per_seq = None
        vn_shape = jax.ShapeDtypeStruct((m, GATE), F32)
        vn_spec = pl.BlockSpec((tm, GATE), lambda i: (i, 0))
    else:
        blocks_per_seq = rows_per_seq // tm
        vn_shape = jax.ShapeDtypeStruct((m // rows_per_seq, CHUNK, GATE), F32)
        vn_spec = pl.BlockSpec((1, CHUNK, GATE), lambda i: (i // blocks_per_seq, 0, 0))
    return pl.pallas_call(
        functools.partial(_gmlp_kernel, tm=tm, blocks_per_seq=blocks_per_seq),
        grid=(m // tm,),
        in_specs=[
            pl.BlockSpec((tm, D_MODEL), lambda i: (i, 0)),
            _const_spec((D_MODEL, 2 * GATE)), _const_spec((1, 2 * GATE)),
            _const_spec((1, GATE)), _const_spec((1, GATE)),
            _const_spec((SGU_GROUPS, CHUNK, CHUNK)), _const_spec((CHUNK, SGU_GROUPS)),
            _const_spec((GATE, D_MODEL)), _const_spec((1, D_MODEL)),
            _const_spec((1, D_MODEL)), _const_spec((1, D_MODEL)),
        ],
        out_specs=[pl.BlockSpec((tm, D_MODEL), lambda i: (i, 0)), vn_spec],
        out_shape=[jax.ShapeDtypeStruct((m, D_MODEL), F32), vn_shape],
        scratch_shapes=[pltpu.VMEM((tm, GATE), BF16), pltpu.VMEM((tm, GATE), BF16)],
        compiler_params=_params(1),
        name="gmlp_layer",
    )(x, w_in, b_in, gv, bv, ws, bs_t, w_out, b_out, g, b)


def _mlp_body(x, w1_ref, w2_ref, g_ref, b_ref, out_ref):
    xb = x.astype(BF16)
    acc = jnp.zeros_like(x)
    for j in range(D_FF // D_MODEL):
        lo, hi = j * D_MODEL, (j + 1) * D_MODEL
        h = jnp.square(jnp.maximum(_dot(xb, w1_ref[:, lo:hi]), 0.0)).astype(BF16)
        acc = acc + _dot(h, w2_ref[lo:hi, :])
    out_ref[...] = _ln(ALPHA * x + acc, g_ref[...], b_ref[...])


def _mlp_kernel(x_ref, w1_ref, w2_ref, g_ref, b_ref, out_ref):
    _mlp_body(x_ref[...], w1_ref, w2_ref, g_ref, b_ref, out_ref)


def _attn_mlp_kernel(x_ref, o_ref, wo_ref, ga_ref, ba_ref, w1_ref, w2_ref, g_ref, b_ref, out_ref):
    x = _ln(ALPHA * x_ref[...] + _dot(o_ref[...].astype(BF16), wo_ref[...]), ga_ref[...], ba_ref[...])
    _mlp_body(x, w1_ref, w2_ref, g_ref, b_ref, out_ref)


def _mlp_layer(x, w1, w2, layer, g, b, *, tm, attn=None):
    m = x.shape[0]
    row_spec = pl.BlockSpec((tm, D_MODEL), lambda i: (i, 0))
    vec = _const_spec((1, D_MODEL))
    one_layer = lambda *shape: pl.BlockSpec((None,) + shape, lambda i: (layer, 0, 0), pipeline_mode=pl.Buffered(1))
    mlp_specs = [one_layer(D_MODEL, D_FF), one_layer(D_FF, D_MODEL), vec, vec]
    if attn is None:
        kern, args, specs = _mlp_kernel, (x, w1, w2, g, b), [row_spec] + mlp_specs
    else:
        o, wo, ga, ba = attn
        kern = _attn_mlp_kernel
        args = (x, o, wo, ga, ba, w1, w2, g, b)
        specs = [row_spec, row_spec, _const_spec((D_MODEL, D_MODEL)), vec, vec] + mlp_specs
    return pl.pallas_call(
        kern, grid=(m // tm,), in_specs=specs, out_specs=row_spec,
        out_shape=jax.ShapeDtypeStruct((m, D_MODEL), F32),
        compiler_params=_params(1), name="mlp_layer",
    )(*args)


def _matmul_kernel(x_ref, w_ref, o_ref, xb_sc):
    @pl.when(pl.program_id(1) == 0)
    def _():
        xb_sc[...] = x_ref[...].astype(BF16)

    o_ref[...] = _dot(xb_sc[...], w_ref[...])


def _matmul(x, w, *, tm, tn):
    m, k = x.shape
    n = w.shape[1]
    return pl.pallas_call(
        _matmul_kernel, grid=(m // tm, n // tn),
        in_specs=[pl.BlockSpec((tm, k), lambda i, j: (i, 0)), pl.BlockSpec((k, tn), lambda i, j: (0, j))],
        out_specs=pl.BlockSpec((tm, tn), lambda i, j: (i, j)),
        out_shape=jax.ShapeDtypeStruct((m, n), F32),
        scratch_shapes=[pltpu.VMEM((tm, k), BF16)],
        compiler_params=_params(2), name="qkv_projection",
    )(x, w)


def _kv_state_kernel(x_ref, wt_ref, o_ref, wb_sc):
    @pl.when(jnp.logical_and(pl.program_id(0) == 0, pl.program_id(1) == 0))
    def _():
        wb_sc[...] = wt_ref[0].astype(BF16)

    o_ref[0] = _dot_nt(wb_sc[...], x_ref[...].astype(BF16))


def _kv_state(x, wkv_t, g, *, batch, seq, rows, tw):
    per_seq = seq // tw
    first = (seq - rows) // tw
    return pl.pallas_call(
        _kv_state_kernel, grid=(batch, rows // tw),
        in_specs=[pl.BlockSpec((tw, D_MODEL), lambda b, j: (b * per_seq + first + j, 0)),
                  pl.BlockSpec((1, 2 * D_MODEL, D_MODEL), lambda b, j: (g, 0, 0), pipeline_mode=pl.Buffered(1))],
        out_specs=pl.BlockSpec((1, 2 * D_MODEL, tw), lambda b, j: (b, 0, j)),
        out_shape=jax.ShapeDtypeStruct((batch, 2 * D_MODEL, rows), F32),
        scratch_shapes=[pltpu.VMEM((2 * D_MODEL, D_MODEL), BF16)],
        compiler_params=_params(2), name="kv_state",
    )(x, wkv_t)


def _kv_state_sample_kernel(x_ref, wt_ref, o_ref, wb_sc):
    @pl.when(pl.program_id(1) == 0)
    def _():
        wb_sc[...] = wt_ref[0].astype(BF16)

    o_ref[0, 0] = _dot_nt(wb_sc[...], x_ref[...].astype(BF16))


def _kv_state_sample(x3, wkv_t):
    n_new, nb, _ = x3.shape
    return pl.pallas_call(
        _kv_state_sample_kernel, grid=(N_BRANCH, n_new),
        in_specs=[pl.BlockSpec((None, nb, D_MODEL), lambda g, i: (i, 0, 0)),
                  pl.BlockSpec((1, 2 * D_MODEL, D_MODEL), lambda g, i: (g, 0, 0))],
        out_specs=pl.BlockSpec((1, 1, 2 * D_MODEL, nb), lambda g, i: (g, i, 0, 0)),
        out_shape=jax.ShapeDtypeStruct((N_BRANCH, n_new, 2 * D_MODEL, nb), F32),
        scratch_shapes=[pltpu.VMEM((2 * D_MODEL, D_MODEL), BF16)],
        compiler_params=_params(2), name="kv_state_sample",
    )(x3, wkv_t)


def _attend_block(q2, ks, vs, bias, no_prev, head_a, in_prev):
    qb = (q2 * Q_SCALE).astype(BF16)
    zero = jnp.zeros_like(qb)
    qs = jnp.concatenate([jnp.where(head_a, qb, zero), jnp.where(head_a, zero, qb)], axis=0)
    s = _dot_nt(qs, ks.astype(BF16)) + bias
    if no_prev is not None:
        s = s + jnp.where(in_prev, no_prev, 0.0)
    m = jnp.max(s, axis=-1, keepdims=True)
    p = jnp.exp(s - m)
    vs_aug = jnp.concatenate([vs.astype(BF16), jnp.ones((2 * KEY_BLOCK, LANES), BF16)], axis=1)
    ov = _dot(p.astype(BF16), vs_aug)
    num2 = jnp.where(head_a, ov[:KEY_BLOCK, :LANES], ov[KEY_BLOCK:, :LANES])
    den2 = jnp.where(head_a, ov[:KEY_BLOCK, LANES:], ov[KEY_BLOCK:, LANES:])
    l2 = jnp.where(head_a, m[:KEY_BLOCK], m[KEY_BLOCK:]) + jnp.log(den2)
    return num2 / den2, l2


def _prompt_attn_kernel(*refs, max_group):
    ins, (steps_ref, step0_ref, o_ref), scr = refs[:15], refs[15:18], refs[18:]
    oscs, lscs, bias_sc = scr[0:3], scr[3:6], scr[6]
    t = pl.program_id(2)
    lane = lax.broadcasted_iota(jnp.int32, (KEY_BLOCK, LANES), 1)
    qrow = lax.broadcasted_iota(jnp.int32, (KEY_BLOCK, LANES), 0)
    head_a = lane < HEAD_DIM
    kcol = lax.broadcasted_iota(jnp.int32, (2 * KEY_BLOCK, 2 * KEY_BLOCK), 1)
    in_prev = kcol < KEY_BLOCK
    no_prev_first = jnp.where(t == 0, NEG, 0.0)

    for g in range(N_BRANCH):
        for hd in range(HEADS_PER_STEP):
            steps = jnp.broadcast_to(steps_ref[g, hd], (KEY_BLOCK, LANES))
            rolled = pltpu.roll(steps, 0, 1, stride=1, stride_axis=0)
            step0 = jnp.broadcast_to(step0_ref[g, hd], (KEY_BLOCK, LANES))
            left = jnp.where(lane >= qrow, rolled, NEG)
            right = jnp.where(lane < qrow, rolled, jnp.where(lane == qrow, step0, NEG))
            bias_sc[g, hd * KEY_BLOCK:(hd + 1) * KEY_BLOCK, :] = jnp.concatenate([left, right], axis=1)

    pooled, loops = [], []
    for g, d in enumerate(DILATIONS):
        q_ref, kc_ref, vc_ref, kp_ref, vp_ref = ins[5 * g:5 * g + 5]
        nqb = Q_TILE // d // KEY_BLOCK

        def rows(start, size, d=d):
            if d == 1:
                return pl.ds(start if isinstance(start, int) else pl.multiple_of(start, KEY_BLOCK), size)
            return pl.ds(start, size, stride=d)

        def load_first(r, q_ref=q_ref, kc_ref=kc_ref, vc_ref=vc_ref, kp_ref=kp_ref, vp_ref=vp_ref, rows=rows):
            sel = rows(r, KEY_BLOCK)
            ks = jnp.concatenate([kp_ref[sel, :], kc_ref[sel, :]], axis=0)
            vs = jnp.concatenate([vp_ref[sel, :], vc_ref[sel, :]], axis=0)
            return sel, q_ref[sel, :], ks, vs

        def load_later(sp, d=d, nqb=nqb, q_ref=q_ref, kc_ref=kc_ref, vc_ref=vc_ref, rows=rows):
            r = sp // (nqb - 1)
            qb = 1 + sp % (nqb - 1)
            q_sel = rows(d * KEY_BLOCK * qb + r, KEY_BLOCK)
            k_sel = rows(d * KEY_BLOCK * (qb - 1) + r, 2 * KEY_BLOCK)
            return q_sel, q_ref[q_sel, :], kc_ref[k_sel, :], vc_ref[k_sel, :]

        if d < max_group:
            pooled += [(g, load_first, r, no_prev_first) for r in range(d)]
        else:
            loops.append((g, d, load_first, no_prev_first))
        if nqb > 1:
            loops.append((g, d * (nqb - 1), load_later, None))

    def attend_group(items):
        loaded = [(g, no_prev) + load(idx) for g, load, idx, no_prev in items]
        done = [(g, sel, _attend_block(q2, ks, vs, bias_sc[g], no_prev, head_a, in_prev))
                for g, no_prev, sel, q2, ks, vs in loaded]
        for g, sel, (o2, l2) in done:
            oscs[g][sel, :] = o2
            lscs[g][sel, :] = l2

    for i in range(0, len(pooled), max_group):
        attend_group(pooled[i:i + max_group])
    for g, n_blocks, load, no_prev in loops:
        group = max(u for u in range(1, max_group + 1) if n_blocks % u == 0)

        def body(it, carry, g=g, load=load, no_prev=no_prev, group=group):
            attend_group([(g, load, it * group + u, no_prev) for u in range(group)])
            return carry

        lax.fori_loop(0, n_blocks // group, body, 0)

    merge_rows = 256
    for c in range(Q_TILE // merge_rows):
        rs = slice(c * merge_rows, (c + 1) * merge_rows)
        ls = [lscs[g][rs, :] for g in range(N_BRANCH)]
        m = jnp.maximum(jnp.maximum(ls[0], ls[1]), ls[2])
        es = [jnp.exp(l - m) for l in ls]
        num = es[0] * oscs[0][rs, :] + es[1] * oscs[1][rs, :] + es[2] * oscs[2][rs, :]
        o_ref[rs, :] = (num / (es[0] + es[1] + es[2])).astype(o_ref.dtype)


def _prompt_attention(qkv, steps_rev, step0, *, batch, seq, max_group=6):
    n_tiles = seq // Q_TILE
    in_specs = []
    for g, d in enumerate(DILATIONS):
        prev_rows = KEY_BLOCK * d
        per_tile = Q_TILE // prev_rows
        q_col = g * N_PAIRS
        k_col = (KV_OFFSET + 2 * g * D_MODEL) // LANES
        v_col = k_col + N_PAIRS

        def cur(col):
            return pl.BlockSpec((Q_TILE, LANES), lambda hp, b, t, col=col: (b * n_tiles + t, col + hp))

        def prev(col, per_tile=per_tile, prev_rows=prev_rows):
            return pl.BlockSpec(
                (prev_rows, LANES),
                lambda hp, b, t, col=col: (jnp.maximum((b * n_tiles + t) * per_tile - 1, 0), col + hp))

        in_specs += [cur(q_col), cur(k_col), cur(v_col), prev(k_col), prev(v_col)]
    head_vec = pl.BlockSpec((N_BRANCH, HEADS_PER_STEP, 1, LANES), lambda hp, b, t: (0, hp, 0, 0))
    in_specs += [head_vec, head_vec]
    return pl.pallas_call(
        functools.partial(_prompt_attn_kernel, max_group=max_group),
        grid=(N_PAIRS, batch, n_tiles),
        in_specs=in_specs,
        out_specs=pl.BlockSpec((Q_TILE, LANES), lambda hp, b, t: (b * n_tiles + t, hp)),
        out_shape=jax.ShapeDtypeStruct((batch * seq, D_MODEL), BF16),
        scratch_shapes=[pltpu.VMEM((Q_TILE, LANES), F32)] * (2 * N_BRANCH)
        + [pltpu.VMEM((N_BRANCH, 2 * KEY_BLOCK, 2 * KEY_BLOCK), F32)],
        compiler_params=_params(3), name="prompt_attention",
    )(*([qkv] * 15), steps_rev, step0)


def _sample_attn_kernel(*refs, n_new, heads):
    qkv_refs, caches, cbias, bn_ref, o_ref = refs[:9], refs[9:12], refs[12:15], refs[15], refs[16]
    j = pl.program_id(1)
    which = pl.program_id(2)
    width = heads * HEAD_DIM
    per_block = NEW_ROWS // n_new

    def own_rows(ref):
        rows = ref[0:n_new, :]
        for w in range(1, per_block):
            rows = jnp.where(which == w, ref[w * n_new:(w + 1) * n_new, :], rows)
        return rows

    pad = jnp.zeros((NEW_ROWS - n_new, width), F32)
    lane_head = lax.broadcasted_iota(jnp.int32, (heads, width), 1) // HEAD_DIM
    own_lanes = lane_head == lax.broadcasted_iota(jnp.int32, (heads, width), 0)

    def per_head_rows(x):
        return jnp.concatenate(
            [jnp.sum(jnp.where(own_lanes, x[i * heads:(i + 1) * heads], 0.0), axis=0, keepdims=True)
             for i in range(n_new)], axis=0)

    outs, lses = [], []
    for g in range(N_BRANCH):
        q = own_rows(qkv_refs[3 * g]) * Q_SCALE
        k_new = jnp.concatenate([own_rows(qkv_refs[3 * g + 1]), pad], axis=0).astype(BF16)
        v_new = jnp.concatenate([own_rows(qkv_refs[3 * g + 2]), pad], axis=0).astype(BF16)
        qbd = jnp.concatenate([jnp.where(own_lanes, q[i:i + 1, :], 0.0) for i in range(n_new)],
                              axis=0).astype(BF16)
        n_pos = caches[g].shape[-1]
        k_t = caches[g][0, 0].reshape(width, n_pos).astype(BF16)
        v_t = caches[g][0, 1].reshape(width, n_pos).astype(BF16)
        s_c = _dot(qbd, k_t) + cbias[g][j]
        s_n = _dot_nt(qbd, k_new) + bn_ref[g, j]
        m = jnp.maximum(jnp.max(s_c, axis=-1, keepdims=True), jnp.max(s_n, axis=-1, keepdims=True))
        p_c = jnp.exp(s_c - m)
        p_n = jnp.exp(s_n - m)
        den = jnp.sum(p_c, axis=-1, keepdims=True) + jnp.sum(p_n, axis=-1, keepdims=True)
        o = (_dot_nt(p_c.astype(BF16), v_t) + _dot(p_n.astype(BF16), v_new)) / den
        outs.append(per_head_rows(o))
        lses.append(per_head_rows(jnp.broadcast_to(m + jnp.log(den), o.shape)))
    m = jnp.maximum(jnp.maximum(lses[0], lses[1]), lses[2])
    es = [jnp.exp(l - m) for l in lses]
    num = es[0] * outs[0] + es[1] * outs[1] + es[2] * outs[2]
    merged = num / (es[0] + es[1] + es[2])
    for w in range(per_block):
        @pl.when(which == w)
        def _(w=w):
            o_ref[w * n_new:(w + 1) * n_new, :] = merged


def _sample_attention(qkv, caches, cache_bias, new_bias, *, n_new, heads_per_step=8):
    hs = heads_per_step
    width = hs * HEAD_DIM
    per_block = NEW_ROWS // n_new
    n_blocks = qkv.shape[0] // NEW_ROWS
    window = lambda col: pl.BlockSpec((NEW_ROWS, width), lambda r, j, w, col=col: (r, col + j))
    per_d = D_MODEL // width
    windows = []
    for g in range(N_BRANCH):
        k_col = (KV_OFFSET + 2 * g * D_MODEL) // width
        windows += [window(g * per_d), window(k_col), window(k_col + per_d)]
    return pl.pallas_call(
        functools.partial(_sample_attn_kernel, n_new=n_new, heads=hs),
        grid=(n_blocks, N_HEADS // hs, per_block),
        in_specs=windows
        + [pl.BlockSpec((1, 2, hs, HEAD_DIM, c.shape[-1]), lambda r, j, w: (r * per_block + w, 0, j, 0, 0))
           for c in caches]
        + [_const_spec(t.shape) for t in cache_bias] + [_const_spec(new_bias.shape)],
        out_specs=pl.BlockSpec((NEW_ROWS, width), lambda r, j, w: (r, j)),
        out_shape=jax.ShapeDtypeStruct((qkv.shape[0], D_MODEL), F32),
        compiler_params=_params(3), name="sample_attention",
    )(*([qkv] * 9), *caches, *cache_bias, new_bias)


def _t5_bucket(dist):
    max_exact = NUM_BUCKETS // 2
    n = jnp.maximum(dist, 1).astype(F32)
    large = max_exact + (jnp.log(n / max_exact) / math.log(MAX_DISTANCE / max_exact)
                         * (NUM_BUCKETS - max_exact)).astype(jnp.int32)
    large = jnp.minimum(large, NUM_BUCKETS - 1)
    return jnp.where(dist < max_exact, dist, large)


def _branch_bias(rel_bias, g):
    n_keys = WINDOWS[g] // DILATIONS[g] + 1
    dist = jnp.arange(n_keys, dtype=jnp.int32) * DILATIONS[g]
    return rel_bias[_t5_bucket(dist)][:, g * N_HEADS:(g + 1) * N_HEADS].T.astype(F32)


def _sample_bias_tables(biases, n_new, hs):
    neg = lambda *shape: jnp.full(shape, NEG, F32)
    by_step = lambda t: t.reshape(N_HEADS // hs, hs, n_new, -1).transpose(0, 2, 1, 3).reshape(
        N_HEADS // hs, n_new * hs, -1)
    cache_tabs, new_tabs = [], []
    for g, d in enumerate(DILATIONS):
        w = WINDOWS[g]
        n_steps = w // d
        b = biases[g]
        rows, new_rows = [], []
        for i in range(n_new):
            first = n_steps + i // d
            hit = b[:, min(first, n_steps):i // d:-1]
            hit = jnp.concatenate([neg(N_HEADS, n_steps - hit.shape[1]), hit], axis=1)
            grid = jnp.where((jnp.arange(d) == i % d)[None, None, :], hit[:, :, None], NEG)
            rows.append(grid.reshape(N_HEADS, w))
            cols = [b[:, (i - j) // d] if (j <= i and (i - j) % d == 0 and j < n_new) else neg(N_HEADS)
                    for j in range(NEW_ROWS)]
            new_rows.append(jnp.stack(cols, axis=1))
        cache_tabs.append(by_step(jnp.stack(rows, axis=1)))
        new_tabs.append(by_step(jnp.stack(new_rows, axis=1)))
    return cache_tabs, jnp.stack(new_tabs)


def kernel(x_prompt, x_sample, cache_kv_w128, cache_kv_w512, cache_kv_w2048, ln_g, ln_b, gm_w_in, gm_b_in,
           gm_ln_g, gm_ln_b, gm_w_s, gm_b_s, gm_w_out, gm_b_out, w_kv, attn_w_q, attn_w_o, rel_bias,
           mlp_w1, mlp_w2):
    batch, seq, _ = x_prompt.shape
    n_samp, n_new, _ = x_sample.shape
    ms = n_samp * n_new
    assert DEPTH == 2 and gm_w_in.shape[0] == 1 and attn_w_q.shape[0] == 1
    assert seq % Q_TILE == 0 and ms % CHUNK == 0 and CHUNK % n_new == 0 and NEW_ROWS % n_new == 0
    assert all(PAST_LEN >= w and w % d == 0 and w // d == KEY_BLOCK for w, d in zip(WINDOWS, DILATIONS))

    bf = lambda w: w.astype(BF16)
    vec = lambda v: v.reshape(1, -1)
    w_in, w_out = bf(gm_w_in[0]), bf(gm_w_out[0])
    w1, w2 = bf(mlp_w1), bf(mlp_w2)
    wqkv = jnp.concatenate([bf(attn_w_q[0]).reshape(D_MODEL, KV_OFFSET),
                            bf(w_kv).reshape(D_MODEL, 2 * KV_OFFSET)], axis=1)
    wo = bf(attn_w_o[0])
    gm_args = (vec(gm_b_in[0]), vec(gm_ln_g[0]), vec(gm_ln_b[0]))
    gm_tail = (w_out, vec(gm_b_out[0]), vec(ln_g[0, 0]), vec(ln_b[0, 0]))

    xp = x_prompt.reshape(batch * seq, D_MODEL)
    xs = x_sample.reshape(ms, D_MODEL)
    x1p, vn_p = _gmlp_layer(xp, w_in, *gm_args, gm_w_s[0], gm_b_s[0].T, *gm_tail, tm=512, rows_per_seq=seq)
    reps = CHUNK // n_new
    same_seq = (jnp.arange(CHUNK)[:, None] // n_new) == (jnp.arange(CHUNK)[None, :] // n_new)
    ws_s = jnp.where(same_seq[None], jnp.tile(gm_w_s[0][:, :n_new, :n_new], (1, reps, reps)), 0.0)
    bs_s = jnp.tile(gm_b_s[0][:, :n_new], (1, reps)).T
    x1s, vn_s = _gmlp_layer(xs, w_in, *gm_args, ws_s, bs_s, *gm_tail, tm=ms, rows_per_seq=None)

    x2p = _mlp_layer(x1p, w1, w2, 0, vec(ln_g[0, 1]), vec(ln_b[0, 1]), tm=512)
    x2s = _mlp_layer(x1s, w1, w2, 0, vec(ln_g[0, 1]), vec(ln_b[0, 1]), tm=ms)

    qkv_p = _matmul(x2p, wqkv, tm=1024, tn=2304)
    qkv_s = _matmul(x2s, wqkv, tm=ms, tn=1536)

    biases = [_branch_bias(rel_bias, g) for g in range(N_BRANCH)]
    steps_rev = jnp.stack([b[:, :0:-1] for b in biases])[:, :, None, :]
    step0 = jnp.stack([jnp.broadcast_to(b[:, :1], (N_HEADS, LANES)) for b in biases])[:, :, None, :]
    o_p = _prompt_attention(qkv_p, steps_rev, step0, batch=batch, seq=seq)

    caches = [jnp.transpose(c, (0, 2, 3, 4, 1)) for c in (cache_kv_w128, cache_kv_w512, cache_kv_w2048)]
    heads_per_step = 8
    cache_bias, new_bias = _sample_bias_tables(biases, n_new, heads_per_step)
    o_s = _sample_attention(qkv_s, caches, cache_bias, new_bias, n_new=n_new, heads_per_step=heads_per_step)

    attn_ln = (vec(ln_g[1, 0]), vec(ln_b[1, 0]))
    y_p = _mlp_layer(x2p, w1, w2, 1, vec(ln_g[1, 1]), vec(ln_b[1, 1]), tm=512, attn=(o_p, wo) + attn_ln)
    y_s = _mlp_layer(x2s, w1, w2, 1, vec(ln_g[1, 1]), vec(ln_b[1, 1]), tm=ms, attn=(o_s, wo) + attn_ln)

    wkv_t = jnp.transpose(w_kv, (1, 2, 3, 4, 0)).reshape(N_BRANCH, 2 * D_MODEL, D_MODEL)
    kv_p_out = []
    for g in range(N_BRANCH):
        rows = min(WINDOWS[g], seq)
        kv_t = _kv_state(x2p, wkv_t, g, batch=batch, seq=seq, rows=rows, tw=min(rows, 512))
        kv_p_out.append(kv_t.reshape(batch, 2, N_HEADS, HEAD_DIM, rows).transpose(0, 4, 1, 2, 3))
    kv_s_t = _kv_state_sample(x2s.reshape(n_samp, n_new, D_MODEL).transpose(1, 0, 2), wkv_t)
    kv_s_out = [kv_s_t[g].reshape(n_new, 2, N_HEADS, HEAD_DIM, n_samp).transpose(4, 0, 1, 2, 3)
                for g in range(N_BRANCH)]
    return (y_p.reshape(batch, seq, D_MODEL), y_s.reshape(n_samp, n_new, D_MODEL),
            vn_p[None], vn_s.reshape(1, n_samp, n_new, GATE),
            kv_p_out[0], kv_s_out[0], kv_p_out[1], kv_s_out[1], kv_p_out[2], kv_s_out[2])
```

```python
import functools
import math

import jax
import jax.numpy as jnp
from jax import lax
from jax.experimental import pallas as pl
from jax.experimental.pallas import tpu as pltpu

D_MODEL = 1024
CHUNK = 128
GATE = 2 * D_MODEL
SGU_GROUPS = 8
SGU_GROUP_DIM = GATE // SGU_GROUPS
WINDOWS = (128, 512, 2048)
DILATIONS = (1, 4, 16)
N_BRANCH = 3
HEAD_DIM = 64
N_HEADS = 16
D_FF = 4 * D_MODEL
NUM_BUCKETS = 32
MAX_DISTANCE = 2048
PAST_LEN = 2048
DEPTH = 2
ALPHA = (2 * DEPTH) ** 0.25
LN_EPS = 1e-5
NEG = -1e30
Q_SCALE = HEAD_DIM ** -0.5

BF16 = jnp.bfloat16
F32 = jnp.float32

VMEM_LIMIT_BYTES = 56 * 1024 * 1024
SAMPLE_ATTN_VMEM_LIMIT_BYTES = 60 * 1024 * 1024
LANES = 128
Q_TILE = 2048
KEY_BLOCK = 128
HEADS_PER_STEP = LANES // HEAD_DIM
N_PAIRS = N_HEADS // HEADS_PER_STEP
NEW_ROWS = 8
QKV_WIDTH = N_BRANCH * 3 * D_MODEL
KV_OFFSET = N_BRANCH * D_MODEL


def _params(n_axes, vmem_limit_bytes=VMEM_LIMIT_BYTES):
    return pltpu.CompilerParams(dimension_semantics=("arbitrary",) * n_axes, vmem_limit_bytes=vmem_limit_bytes)


def _const_spec(shape):
    return pl.BlockSpec(shape, lambda *_: (0,) * len(shape), pipeline_mode=pl.Buffered(1))


def _ln(x, g, b):
    mu = jnp.mean(x, axis=-1, keepdims=True)
    xc = x - mu
    var = jnp.mean(xc * xc, axis=-1, keepdims=True)
    return xc * lax.rsqrt(var + LN_EPS) * g + b


def _dot(a, b):
    return jnp.dot(a, b, preferred_element_type=F32)


def _dot_nt(a, b):
    return lax.dot_general(a, b, (((1,), (1,)), ((), ())), preferred_element_type=F32)


def _gmlp_kernel(x_ref, w_in_ref, b_in_ref, gv_ref, bv_ref, ws_ref, bs_ref, w_out_ref, b_out_ref,
                 g_ref, b_ref, x1_ref, vn_ref, vnb_sc, gact_sc, *, tm, blocks_per_seq):
    x = x_ref[...]
    xb = x.astype(BF16)
    zv = _dot(xb, w_in_ref[:, GATE:]) + b_in_ref[:, GATE:]
    vn = _ln(jax.nn.gelu(zv), gv_ref[...], bv_ref[...])
    vnb_sc[...] = vn.astype(BF16)

    if blocks_per_seq is None:
        vn_ref[...] = vn
    else:
        @pl.when(pl.program_id(0) % blocks_per_seq == blocks_per_seq - 1)
        def _():
            vn_ref[0] = vn[tm - CHUNK:, :]

    row = lax.broadcasted_iota(jnp.int32, (CHUNK, CHUNK), 0)
    col = lax.broadcasted_iota(jnp.int32, (CHUNK, CHUNK), 1)
    causal = row >= col
    for g in range(SGU_GROUPS):
        lo, hi = g * SGU_GROUP_DIM, (g + 1) * SGU_GROUP_DIM
        zu = jax.nn.gelu(_dot(xb, w_in_ref[:, lo:hi]) + b_in_ref[:, lo:hi])
        wg = jnp.where(causal, ws_ref[g], 0.0).astype(BF16)
        bsg = bs_ref[:, g:g + 1]
        for c in range(tm // CHUNK):
            r0, r1 = c * CHUNK, (c + 1) * CHUNK
            s = _dot(wg, vnb_sc[r0:r1, lo:hi]) + bsg
            gact_sc[r0:r1, lo:hi] = (zu[r0:r1] * s).astype(BF16)
    y = _dot(gact_sc[...], w_out_ref[...]) + b_out_ref[...]
    x1_ref[...] = _ln(ALPHA * x + y, g_ref[...], b_ref[...])


def _gmlp_layer(x, w_in, b_in, gv, bv, ws, bs_t, w_out, b_out, g, b, *, tm, rows_per_seq):
    m = x.shape[0]
    if rows_per_seq is None:
        blocks_per_seq = None
        vn_shape = jax.ShapeDtypeStruct((m, GATE), F32)
        vn_spec = pl.BlockSpec((tm, GATE), lambda i: (i, 0))
    else:
        blocks_per_seq = rows_per_seq // tm
        vn_shape = jax.ShapeDtypeStruct((m // rows_per_seq, CHUNK, GATE), F32)
        vn_spec = pl.BlockSpec((1, CHUNK, GATE), lambda i: (i // blocks_per_seq, 0, 0))
    return pl.pallas_call(
        functools.partial(_gmlp_kernel, tm=tm, blocks_per_seq=blocks_per_seq),
        grid=(m // tm,),
        in_specs=[
            pl.BlockSpec((tm, D_MODEL), lambda i: (i, 0)),
            _const_spec((D_MODEL, 2 * GATE)), _const_spec((1, 2 * GATE)),
            _const_spec((1, GATE)), _const_spec((1, GATE)),
            _const_spec((SGU_GROUPS, CHUNK, CHUNK)), _const_spec((CHUNK, SGU_GROUPS)),
            _const_spec((GATE, D_MODEL)), _const_spec((1, D_MODEL)),
            _const_spec((1, D_MODEL)), _const_spec((1, D_MODEL)),
        ],
        out_specs=[pl.BlockSpec((tm, D_MODEL), lambda i: (i, 0)), vn_spec],
        out_shape=[jax.ShapeDtypeStruct((m, D_MODEL), F32), vn_shape],
        scratch_shapes=[pltpu.VMEM((tm, GATE), BF16), pltpu.VMEM((tm, GATE), BF16)],
        compiler_params=_params(1),
        name="gmlp_layer",
    )(x, w_in, b_in, gv, bv, ws, bs_t, w_out, b_out, g, b)


def _mlp_body(x, w1_ref, w2_ref, g_ref, b_ref, out_ref):
    xb = x.astype(BF16)
    acc = jnp.zeros_like(x)
    for j in range(D_FF // D_MODEL):
        lo, hi = j * D_MODEL, (j + 1) * D_MODEL
        h = jnp.square(jnp.maximum(_dot(xb, w1_ref[:, lo:hi]), 0.0)).astype(BF16)
        acc = acc + _dot(h, w2_ref[lo:hi, :])
    out_ref[...] = _ln(ALPHA * x + acc, g_ref[...], b_ref[...])


def _mlp_kernel(x_ref, w1_ref, w2_ref, g_ref, b_ref, out_ref):
    _mlp_body(x_ref[...], w1_ref, w2_ref, g_ref, b_ref, out_ref)


def _attn_mlp_kernel(x_ref, o_ref, wo_ref, ga_ref, ba_ref, w1_ref, w2_ref, g_ref, b_ref, out_ref):
    x = _ln(ALPHA * x_ref[...] + _dot(o_ref[...].astype(BF16), wo_ref[...]), ga_ref[...], ba_ref[...])
    _mlp_body(x, w1_ref, w2_ref, g_ref, b_ref, out_ref)


def _mlp_layer(x, w1, w2, layer, g, b, *, tm, attn=None):
    m = x.shape[0]
    row_spec = pl.BlockSpec((tm, D_MODEL), lambda i: (i, 0))
    vec = _const_spec((1, D_MODEL))
    one_layer = lambda *shape: pl.BlockSpec((None,) + shape, lambda i: (layer, 0, 0), pipeline_mode=pl.Buffered(1))
    mlp_specs = [one_layer(D_MODEL, D_FF), one_layer(D_FF, D_MODEL), vec, vec]
    if attn is None:
        kern, args, specs = _mlp_kernel, (x, w1, w2, g, b), [row_spec] + mlp_specs
    else:
        o, wo, ga, ba = attn
        kern = _attn_mlp_kernel
        args = (x, o, wo, ga, ba, w1, w2, g, b)
        specs = [row_spec, row_spec, _const_spec((D_MODEL, D_MODEL)), vec, vec] + mlp_specs
    return pl.pallas_call(
        kern, grid=(m // tm,), in_specs=specs, out_specs=row_spec,
        out_shape=jax.ShapeDtypeStruct((m, D_MODEL), F32),
        compiler_params=_params(1), name="mlp_layer",
    )(*args)


def _matmul_kernel(x_ref, w_ref, o_ref, xb_sc):
    @pl.when(pl.program_id(1) == 0)
    def _():
        xb_sc[...] = x_ref[...].astype(BF16)

    o_ref[...] = _dot(xb_sc[...], w_ref[...])


def _matmul(x, w, *, tm, tn):
    m, k = x.shape
    n = w.shape[1]
    return pl.pallas_call(
        _matmul_kernel, grid=(m // tm, n // tn),
        in_specs=[pl.BlockSpec((tm, k), lambda i, j: (i, 0)), pl.BlockSpec((k, tn), lambda i, j: (0, j))],
        out_specs=pl.BlockSpec((tm, tn), lambda i, j: (i, j)),
        out_shape=jax.ShapeDtypeStruct((m, n), F32),
        scratch_shapes=[pltpu.VMEM((tm, k), BF16)],
        compiler_params=_params(2), name="qkv_projection",
    )(x, w)


def _kv_state_kernel(x_ref, wt_ref, o_ref):
    o_ref[...] = _dot_nt(wt_ref[...], x_ref[...].astype(BF16))


def _kv_state(x, wkv_t, g, *, batch, seq, rows, tw):
    per_seq = seq // tw
    first = (seq - rows) // tw
    return pl.pallas_call(
        _kv_state_kernel, grid=(batch, rows // tw),
        in_specs=[pl.BlockSpec((tw, D_MODEL), lambda b, j: (b * per_seq + first + j, 0)),
                  pl.BlockSpec((None, 2 * D_MODEL, D_MODEL), lambda b, j: (g, 0, 0), pipeline_mode=pl.Buffered(1))],
        out_specs=pl.BlockSpec((None, 2 * D_MODEL, tw), lambda b, j: (b, 0, j)),
        out_shape=jax.ShapeDtypeStruct((batch, 2 * D_MODEL, rows), F32),
        compiler_params=_params(2), name="kv_state",
    )(x, wkv_t)


def _kv_state_sample(x3, wkv_t):
    n_new, nb, _ = x3.shape
    return pl.pallas_call(
        _kv_state_kernel, grid=(N_BRANCH, n_new),
        in_specs=[pl.BlockSpec((None, nb, D_MODEL), lambda g, i: (i, 0, 0)),
                  pl.BlockSpec((None, 2 * D_MODEL, D_MODEL), lambda g, i: (g, 0, 0))],
        out_specs=pl.BlockSpec((None, None, 2 * D_MODEL, nb), lambda g, i: (g, i, 0, 0)),
        out_shape=jax.ShapeDtypeStruct((N_BRANCH, n_new, 2 * D_MODEL, nb), F32),
        compiler_params=_params(2), name="kv_state_sample",
    )(x3, wkv_t)


def _attend_block(q2, ks, vs, bias, no_prev, head_a, in_prev):
    qb = (q2 * Q_SCALE).astype(BF16)
    zero = jnp.zeros_like(qb)
    qs = jnp.concatenate([jnp.where(head_a, qb, zero), jnp.where(head_a, zero, qb)], axis=0)
    s = _dot_nt(qs, ks.astype(BF16)) + bias
    if no_prev is not None:
        s = s + jnp.where(in_prev, no_prev, 0.0)
    m = jnp.max(s, axis=-1, keepdims=True)
    p = jnp.exp(s - m)
    vs_aug = jnp.concatenate([vs.astype(BF16), jnp.ones((2 * KEY_BLOCK, LANES), BF16)], axis=1)
    ov = _dot(p.astype(BF16), vs_aug)
    num2 = jnp.where(head_a, ov[:KEY_BLOCK, :LANES], ov[KEY_BLOCK:, :LANES])
    den2 = jnp.where(head_a, ov[:KEY_BLOCK, LANES:], ov[KEY_BLOCK:, LANES:])
    l2 = jnp.where(head_a, m[:KEY_BLOCK], m[KEY_BLOCK:]) + jnp.log(den2)
    return num2 / den2, l2


def _prompt_attn_kernel(*refs, max_group):
    ins, (steps_ref, step0_ref, o_ref), scr = refs[:15], refs[15:18], refs[18:]
    oscs, lscs, bias_sc = scr[0:3], scr[3:6], scr[6]
    t = pl.program_id(2)
    lane = lax.broadcasted_iota(jnp.int32, (KEY_BLOCK, LANES), 1)
    qrow = lax.broadcasted_iota(jnp.int32, (KEY_BLOCK, LANES), 0)
    head_a = lane < HEAD_DIM
    kcol = lax.broadcasted_iota(jnp.int32, (2 * KEY_BLOCK, 2 * KEY_BLOCK), 1)
    in_prev = kcol < KEY_BLOCK
    no_prev_first = jnp.where(t == 0, NEG, 0.0)

    @pl.when(jnp.logical_and(pl.program_id(1) == 0, t == 0))
    def _():
        for g in range(N_BRANCH):
            for hd in range(HEADS_PER_STEP):
                steps = jnp.broadcast_to(steps_ref[g, hd], (KEY_BLOCK, LANES))
                rolled = pltpu.roll(steps, 0, 1, stride=1, stride_axis=0)
                step0 = jnp.broadcast_to(step0_ref[g, hd], (KEY_BLOCK, LANES))
                left = jnp.where(lane >= qrow, rolled, NEG)
                right = jnp.where(lane < qrow, rolled, jnp.where(lane == qrow, step0, NEG))
                bias_sc[g, hd * KEY_BLOCK:(hd + 1) * KEY_BLOCK, :] = jnp.concatenate([left, right], axis=1)

    pooled, loops = [], []
    for g, d in enumerate(DILATIONS):
        q_ref, kc_ref, vc_ref, kp_ref, vp_ref = ins[5 * g:5 * g + 5]
        nqb = Q_TILE // d // KEY_BLOCK

        def rows(start, size, d=d):
            if d == 1:
                return pl.ds(start if isinstance(start, int) else pl.multiple_of(start, KEY_BLOCK), size)
            return pl.ds(start, size, stride=d)

        def load_first(r, q_ref=q_ref, kc_ref=kc_ref, vc_ref=vc_ref, kp_ref=kp_ref, vp_ref=vp_ref, rows=rows):
            sel = rows(r, KEY_BLOCK)
            ks = jnp.concatenate([kp_ref[sel, :], kc_ref[sel, :]], axis=0)
            vs = jnp.concatenate([vp_ref[sel, :], vc_ref[sel, :]], axis=0)
            return sel, q_ref[sel, :], ks, vs

        def load_later(sp, d=d, nqb=nqb, q_ref=q_ref, kc_ref=kc_ref, vc_ref=vc_ref, rows=rows):
            r = sp // (nqb - 1)
            qb = 1 + sp % (nqb - 1)
            q_sel = rows(d * KEY_BLOCK * qb + r, KEY_BLOCK)
            k_sel = rows(d * KEY_BLOCK * (qb - 1) + r, 2 * KEY_BLOCK)
            return q_sel, q_ref[q_sel, :], kc_ref[k_sel, :], vc_ref[k_sel, :]

        if d < max_group:
            pooled += [(g, load_first, r, no_prev_first) for r in range(d)]
        else:
            loops.append((g, d, load_first, no_prev_first))
        if nqb > 1:
            loops.append((g, d * (nqb - 1), load_later, None))

    def attend_group(items):
        loaded = [(g, no_prev) + load(idx) for g, load, idx, no_prev in items]
        done = [(g, sel, _attend_block(q2, ks, vs, bias_sc[g], no_prev, head_a, in_prev))
                for g, no_prev, sel, q2, ks, vs in loaded]
        for g, sel, (o2, l2) in done:
            oscs[g][sel, :] = o2
            lscs[g][sel, :] = l2

    for i in range(0, len(pooled), max_group):
        attend_group(pooled[i:i + max_group])
    for g, n_blocks, load, no_prev in loops:
        group = max(u for u in range(1, max_group + 1) if n_blocks % u == 0)

        def body(it, carry, g=g, load=load, no_prev=no_prev, group=group):
            attend_group([(g, load, it * group + u, no_prev) for u in range(group)])
            return carry

        lax.fori_loop(0, n_blocks // group, body, 0)

    merge_rows = 256
    for c in range(Q_TILE // merge_rows):
        rs = slice(c * merge_rows, (c + 1) * merge_rows)
        ls = [lscs[g][rs, :] for g in range(N_BRANCH)]
        m = jnp.maximum(jnp.maximum(ls[0], ls[1]), ls[2])
        es = [jnp.exp(l - m) for l in ls]
        num = es[0] * oscs[0][rs, :] + es[1] * oscs[1][rs, :] + es[2] * oscs[2][rs, :]
        o_ref[rs, :] = (num / (es[0] + es[1] + es[2])).astype(o_ref.dtype)


def _prompt_attention(qkv, steps_rev, step0, *, batch, seq, max_group=6):
    n_tiles = seq // Q_TILE
    in_specs = []
    for g, d in enumerate(DILATIONS):
        prev_rows = KEY_BLOCK * d
        per_tile = Q_TILE // prev_rows
        q_col = g * N_PAIRS
        k_col = (KV_OFFSET + 2 * g * D_MODEL) // LANES
        v_col = k_col + N_PAIRS

        def cur(col):
            return pl.BlockSpec((Q_TILE, LANES), lambda hp, b, t, col=col: (b * n_tiles + t, col + hp))

        def prev(col, per_tile=per_tile, prev_rows=prev_rows):
            return pl.BlockSpec(
                (prev_rows, LANES),
                lambda hp, b, t, col=col: (jnp.maximum((b * n_tiles + t) * per_tile - 1, 0), col + hp))

        in_specs += [cur(q_col), cur(k_col), cur(v_col), prev(k_col), prev(v_col)]
    head_vec = pl.BlockSpec((N_BRANCH, HEADS_PER_STEP, 1, LANES), lambda hp, b, t: (0, hp, 0, 0))
    in_specs += [head_vec, head_vec]
    return pl.pallas_call(
        functools.partial(_prompt_attn_kernel, max_group=max_group),
        grid=(N_PAIRS, batch, n_tiles),
        in_specs=in_specs,
        out_specs=pl.BlockSpec((Q_TILE, LANES), lambda hp, b, t: (b * n_tiles + t, hp)),
        out_shape=jax.ShapeDtypeStruct((batch * seq, D_MODEL), BF16),
        scratch_shapes=[pltpu.VMEM((Q_TILE, LANES), F32)] * (2 * N_BRANCH)
        + [pltpu.VMEM((N_BRANCH, 2 * KEY_BLOCK, 2 * KEY_BLOCK), F32)],
        compiler_params=_params(3), name="prompt_attention",
    )(*([qkv] * 15), steps_rev, step0)


def _sample_attn_kernel(*refs, n_new, heads):
    qkv_refs, caches, cbias, bn_ref, o_ref = refs[:9], refs[9:12], refs[12:15], refs[15], refs[16]
    j = pl.program_id(1)
    which = pl.program_id(2)
    width = heads * HEAD_DIM
    per_block = NEW_ROWS // n_new

    def own_rows(ref):
        rows = ref[0:n_new, :]
        for w in range(1, per_block):
            rows = jnp.where(which == w, ref[w * n_new:(w + 1) * n_new, :], rows)
        return rows

    pad = jnp.zeros((NEW_ROWS - n_new, width), F32)
    lane_head = lax.broadcasted_iota(jnp.int32, (heads, width), 1) // HEAD_DIM
    own_lanes = lane_head == lax.broadcasted_iota(jnp.int32, (heads, width), 0)

    def per_head_rows(x):
        return jnp.concatenate(
            [jnp.sum(jnp.where(own_lanes, x[i * heads:(i + 1) * heads], 0.0), axis=0, keepdims=True)
             for i in range(n_new)], axis=0)

    outs, lses = [], []
    for g in range(N_BRANCH):
        q = own_rows(qkv_refs[3 * g]) * Q_SCALE
        k_new = jnp.concatenate([own_rows(qkv_refs[3 * g + 1]), pad], axis=0).astype(BF16)
        v_new = jnp.concatenate([own_rows(qkv_refs[3 * g + 2]), pad], axis=0).astype(BF16)
        qbd = jnp.concatenate([jnp.where(own_lanes, q[i:i + 1, :], 0.0) for i in range(n_new)],
                              axis=0).astype(BF16)
        n_pos = caches[g].shape[-1]
        k_t = caches[g][0, 0].reshape(width, n_pos).astype(BF16)
        v_t = caches[g][0, 1].reshape(width, n_pos).astype(BF16)
        s_c = _dot(qbd, k_t) + cbias[g][j]
        s_n = _dot_nt(qbd, k_new) + bn_ref[g, j]
        m = jnp.maximum(jnp.max(s_c, axis=-1, keepdims=True), jnp.max(s_n, axis=-1, keepdims=True))
        p_c = jnp.exp(s_c - m)
        p_n = jnp.exp(s_n - m)
        den = jnp.sum(p_c, axis=-1, keepdims=True) + jnp.sum(p_n, axis=-1, keepdims=True)
        o = (_dot_nt(p_c.astype(BF16), v_t) + _dot(p_n.astype(BF16), v_new)) / den
        outs.append(per_head_rows(o))
        lses.append(per_head_rows(jnp.broadcast_to(m + jnp.log(den), o.shape)))
    m = jnp.maximum(jnp.maximum(lses[0], lses[1]), lses[2])
    es = [jnp.exp(l - m) for l in lses]
    num = es[0] * outs[0] + es[1] * outs[1] + es[2] * outs[2]
    merged = num / (es[0] + es[1] + es[2])
    for w in range(per_block):
        @pl.when(which == w)
        def _(w=w):
            o_ref[w * n_new:(w + 1) * n_new, :] = merged


def _sample_attention(qkv, caches, cache_bias, new_bias, *, n_new, heads_per_step=8):
    hs = heads_per_step
    width = hs * HEAD_DIM
    per_block = NEW_ROWS // n_new
    n_blocks = qkv.shape[0] // NEW_ROWS
    window = lambda col: pl.BlockSpec((NEW_ROWS, width), lambda r, j, w, col=col: (r, col + j))
    per_d = D_MODEL // width
    windows = []
    for g in range(N_BRANCH):
        k_col = (KV_OFFSET + 2 * g * D_MODEL) // width
        windows += [window(g * per_d), window(k_col), window(k_col + per_d)]
    return pl.pallas_call(
        functools.partial(_sample_attn_kernel, n_new=n_new, heads=hs),
        grid=(n_blocks, N_HEADS // hs, per_block),
        in_specs=windows
        + [pl.BlockSpec((1, 2, hs, HEAD_DIM, c.shape[-1]), lambda r, j, w: (r * per_block + w, 0, j, 0, 0))
           for c in caches]
        + [_const_spec(t.shape) for t in cache_bias] + [_const_spec(new_bias.shape)],
        out_specs=pl.BlockSpec((NEW_ROWS, width), lambda r, j, w: (r, j)),
        out_shape=jax.ShapeDtypeStruct((qkv.shape[0], D_MODEL), F32),
        compiler_params=_params(3, SAMPLE_ATTN_VMEM_LIMIT_BYTES), name="sample_attention",
    )(*([qkv] * 9), *caches, *cache_bias, new_bias)


def _t5_bucket(dist):
    max_exact = NUM_BUCKETS // 2
    n = jnp.maximum(dist, 1).astype(F32)
    large = max_exact + (jnp.log(n / max_exact) / math.log(MAX_DISTANCE / max_exact)
                         * (NUM_BUCKETS - max_exact)).astype(jnp.int32)
    large = jnp.minimum(large, NUM_BUCKETS - 1)
    return jnp.where(dist < max_exact, dist, large)


def _branch_bias(rel_bias, g):
    n_keys = WINDOWS[g] // DILATIONS[g] + 1
    dist = jnp.arange(n_keys, dtype=jnp.int32) * DILATIONS[g]
    return rel_bias[_t5_bucket(dist)][:, g * N_HEADS:(g + 1) * N_HEADS].T.astype(F32)


def _sample_bias_tables(biases, n_new, hs):
    neg = lambda *shape: jnp.full(shape, NEG, F32)
    by_step = lambda t: t.reshape(N_HEADS // hs, hs, n_new, -1).transpose(0, 2, 1, 3).reshape(
        N_HEADS // hs, n_new * hs, -1)
    cache_tabs, new_tabs = [], []
    for g, d in enumerate(DILATIONS):
        w = WINDOWS[g]
        n_steps = w // d
        b = biases[g]
        rows, new_rows = [], []
        for i in range(n_new):
            first = n_steps + i // d
            hit = b[:, min(first, n_steps):i // d:-1]
            hit = jnp.concatenate([neg(N_HEADS, n_steps - hit.shape[1]), hit], axis=1)
            grid = jnp.where((jnp.arange(d) == i % d)[None, None, :], hit[:, :, None], NEG)
            rows.append(grid.reshape(N_HEADS, w))
            cols = [b[:, (i - j) // d] if (j <= i and (i - j) % d == 0 and j < n_new) else neg(N_HEADS)
                    for j in range(NEW_ROWS)]
            new_rows.append(jnp.stack(cols, axis=1))
        cache_tabs.append(by_step(jnp.stack(rows, axis=1)))
        new_tabs.append(by_step(jnp.stack(new_rows, axis=1)))
    return cache_tabs, jnp.stack(new_tabs)


def kernel(x_prompt, x_sample, cache_kv_w128, cache_kv_w512, cache_kv_w2048, ln_g, ln_b, gm_w_in, gm_b_in,
           gm_ln_g, gm_ln_b, gm_w_s, gm_b_s, gm_w_out, gm_b_out, w_kv, attn_w_q, attn_w_o, rel_bias,
           mlp_w1, mlp_w2):
    batch, seq, _ = x_prompt.shape
    n_samp, n_new, _ = x_sample.shape
    ms = n_samp * n_new
    assert DEPTH == 2 and gm_w_in.shape[0] == 1 and attn_w_q.shape[0] == 1
    assert seq % Q_TILE == 0 and ms % CHUNK == 0 and CHUNK % n_new == 0 and NEW_ROWS % n_new == 0
    assert all(PAST_LEN >= w and w % d == 0 and w // d == KEY_BLOCK for w, d in zip(WINDOWS, DILATIONS))

    bf = lambda w: w.astype(BF16)
    vec = lambda v: v.reshape(1, -1)
    w_in, w_out = bf(gm_w_in[0]), bf(gm_w_out[0])
    w1, w2 = bf(mlp_w1), bf(mlp_w2)
    wqkv = jnp.concatenate([bf(attn_w_q[0]).reshape(D_MODEL, KV_OFFSET),
                            bf(w_kv).reshape(D_MODEL, 2 * KV_OFFSET)], axis=1)
    wo = bf(attn_w_o[0])
    gm_args = (vec(gm_b_in[0]), vec(gm_ln_g[0]), vec(gm_ln_b[0]))
    gm_tail = (w_out, vec(gm_b_out[0]), vec(ln_g[0, 0]), vec(ln_b[0, 0]))

    xp = x_prompt.reshape(batch * seq, D_MODEL)
    xs = x_sample.reshape(ms, D_MODEL)
    x1p, vn_p = _gmlp_layer(xp, w_in, *gm_args, gm_w_s[0], gm_b_s[0].T, *gm_tail, tm=512, rows_per_seq=seq)
    reps = CHUNK // n_new
    same_seq = (jnp.arange(CHUNK)[:, None] // n_new) == (jnp.arange(CHUNK)[None, :] // n_new)
    ws_s = jnp.where(same_seq[None], jnp.tile(gm_w_s[0][:, :n_new, :n_new], (1, reps, reps)), 0.0)
    bs_s = jnp.tile(gm_b_s[0][:, :n_new], (1, reps)).T
    x1s, vn_s = _gmlp_layer(xs, w_in, *gm_args, ws_s, bs_s, *gm_tail, tm=ms, rows_per_seq=None)

    x2p = _mlp_layer(x1p, w1, w2, 0, vec(ln_g[0, 1]), vec(ln_b[0, 1]), tm=512)
    x2s = _mlp_layer(x1s, w1, w2, 0, vec(ln_g[0, 1]), vec(ln_b[0, 1]), tm=ms)

    qkv_p = _matmul(x2p, wqkv, tm=1024, tn=2304)
    qkv_s = _matmul(x2s, wqkv, tm=ms, tn=1536)

    biases = [_branch_bias(rel_bias, g) for g in range(N_BRANCH)]
    steps_rev = jnp.stack([b[:, :0:-1] for b in biases])[:, :, None, :]
    step0 = jnp.stack([jnp.broadcast_to(b[:, :1], (N_HEADS, LANES)) for b in biases])[:, :, None, :]
    o_p = _prompt_attention(qkv_p, steps_rev, step0, batch=batch, seq=seq)

    caches = [jnp.transpose(c, (0, 2, 3, 4, 1)) for c in (cache_kv_w128, cache_kv_w512, cache_kv_w2048)]
    heads_per_step = N_HEADS
    cache_bias, new_bias = _sample_bias_tables(biases, n_new, heads_per_step)
    o_s = _sample_attention(qkv_s, caches, cache_bias, new_bias, n_new=n_new, heads_per_step=heads_per_step)

    attn_ln = (vec(ln_g[1, 0]), vec(ln_b[1, 0]))
    y_p = _mlp_layer(x2p, w1, w2, 1, vec(ln_g[1, 1]), vec(ln_b[1, 1]), tm=512, attn=(o_p, wo) + attn_ln)
    y_s = _mlp_layer(x2s, w1, w2, 1, vec(ln_g[1, 1]), vec(ln_b[1, 1]), tm=ms, attn=(o_s, wo) + attn_ln)

    wkv_t = jnp.transpose(bf(w_kv), (1, 2, 3, 4, 0)).reshape(N_BRANCH, 2 * D_MODEL, D_MODEL)
    kv_p_out = []
    for g in range(N_BRANCH):
        rows = min(WINDOWS[g], seq)
        kv_t = _kv_state(x2p, wkv_t, g, batch=batch, seq=seq, rows=rows, tw=min(rows, 512))
        kv_p_out.append(kv_t.reshape(batch, 2, N_HEADS, HEAD_DIM, rows).transpose(0, 4, 1, 2, 3))
    kv_s_t = _kv_state_sample(x2s.reshape(n_samp, n_new, D_MODEL).transpose(1, 0, 2), wkv_t)
    kv_s_out = [kv_s_t[g].reshape(n_new, 2, N_HEADS, HEAD_DIM, n_samp).transpose(4, 0, 1, 2, 3)
                for g in range(N_BRANCH)]
    return (y_p.reshape(batch, seq, D_MODEL), y_s.reshape(n_samp, n_new, D_MODEL),
            vn_p[None], vn_s.reshape(1, n_samp, n_new, GATE),
            kv_p_out[0], kv_s_out[0], kv_p_out[1], kv_s_out[1], kv_p_out[2], kv_s_out[2])
```

```python
import functools
import math

import jax
import jax.numpy as jnp
from jax import lax
from jax.experimental import pallas as pl
from jax.experimental.pallas import tpu as pltpu

D_MODEL = 1024
CHUNK = 128
GATE = 2 * D_MODEL
SGU_GROUPS = 8
SGU_GROUP_DIM = GATE // SGU_GROUPS
WINDOWS = (128, 512, 2048)
DILATIONS = (1, 4, 16)
N_BRANCH = 3
HEAD_DIM = 64
N_HEADS = 16
D_FF = 4 * D_MODEL
NUM_BUCKETS = 32
MAX_DISTANCE = 2048
PAST_LEN = 2048
DEPTH = 2
ALPHA = (2 * DEPTH) ** 0.25
LN_EPS = 1e-5
NEG = -1e30
Q_SCALE = HEAD_DIM ** -0.5

BF16 = jnp.bfloat16
F32 = jnp.float32

VMEM_LIMIT_BYTES = 56 * 1024 * 1024
SAMPLE_ATTN_VMEM_LIMIT_BYTES = 60 * 1024 * 1024
LANES = 128
Q_TILE = 2048
KEY_BLOCK = 128
HEADS_PER_STEP = LANES // HEAD_DIM
N_PAIRS = N_HEADS // HEADS_PER_STEP
NEW_ROWS = 8
QKV_WIDTH = N_BRANCH * 3 * D_MODEL
KV_OFFSET = N_BRANCH * D_MODEL


def _params(n_axes, vmem_limit_bytes=VMEM_LIMIT_BYTES):
    return pltpu.CompilerParams(dimension_semantics=("arbitrary",) * n_axes, vmem_limit_bytes=vmem_limit_bytes)


def _const_spec(shape):
    return pl.BlockSpec(shape, lambda *_: (0,) * len(shape), pipeline_mode=pl.Buffered(1))


def _ln(x, g, b):
    mu = jnp.mean(x, axis=-1, keepdims=True)
    xc = x - mu
    var = jnp.mean(xc * xc, axis=-1, keepdims=True)
    return xc * lax.rsqrt(var + LN_EPS) * g + b


def _dot(a, b):
    return jnp.dot(a, b, preferred_element_type=F32)


def _dot_nt(a, b):
    return lax.dot_general(a, b, (((1,), (1,)), ((), ())), preferred_element_type=F32)


def _gmlp_kernel(x_ref, w_in_ref, b_in_ref, gv_ref, bv_ref, ws_ref, bs_ref, w_out_ref, b_out_ref,
                 g_ref, b_ref, x1_ref, vn_ref, vnb_sc, gact_sc, *, tm, blocks_per_seq):
    x = x_ref[...]
    xb = x.astype(BF16)
    zv = _dot(xb, w_in_ref[:, GATE:]) + b_in_ref[:, GATE:]
    vn = _ln(jax.nn.gelu(zv), gv_ref[...], bv_ref[...])
    vnb_sc[...] = vn.astype(BF16)

    if blocks_per_seq is None:
        vn_ref[...] = vn
    else:
        @pl.when(pl.program_id(0) % blocks_per_seq == blocks_per_seq - 1)
        def _():
            vn_ref[0] = vn[tm - CHUNK:, :]

    row = lax.broadcasted_iota(jnp.int32, (CHUNK, CHUNK), 0)
    col = lax.broadcasted_iota(jnp.int32, (CHUNK, CHUNK), 1)
    causal = row >= col
    for g in range(SGU_GROUPS):
        lo, hi = g * SGU_GROUP_DIM, (g + 1) * SGU_GROUP_DIM
        zu = jax.nn.gelu(_dot(xb, w_in_ref[:, lo:hi]) + b_in_ref[:, lo:hi])
        wg = jnp.where(causal, ws_ref[g], 0.0).astype(BF16)
        bsg = bs_ref[:, g:g + 1]
        for c in range(tm // CHUNK):
            r0, r1 = c * CHUNK, (c + 1) * CHUNK
            s = _dot(wg, vnb_sc[r0:r1, lo:hi]) + bsg
            gact_sc[r0:r1, lo:hi] = (zu[r0:r1] * s).astype(BF16)
    y = _dot(gact_sc[...], w_out_ref[...]) + b_out_ref[...]
    x1_ref[...] = _ln(ALPHA * x + y, g_ref[...], b_ref[...])


def _gmlp_layer(x, w_in, b_in, gv, bv, ws, bs_t, w_out, b_out, g, b, *, tm, rows_per_seq):
    m = x.shape[0]
    if rows_per_seq is None:
        blocks_per_seq = None
        vn_shape = jax.ShapeDtypeStruct((m, GATE), F32)
        vn_spec = pl.BlockSpec((tm, GATE), lambda i: (i, 0))
    else:
        blocks_per_seq = rows_per_seq // tm
        vn_shape = jax.ShapeDtypeStruct((m // rows_per_seq, CHUNK, GATE), F32)
        vn_spec = pl.BlockSpec((1, CHUNK, GATE), lambda i: (i // blocks_per_seq, 0, 0))
    return pl.pallas_call(
        functools.partial(_gmlp_kernel, tm=tm, blocks_per_seq=blocks_per_seq),
        grid=(m // tm,),
        in_specs=[
            pl.BlockSpec((tm, D_MODEL), lambda i: (i, 0)),
            _const_spec((D_MODEL, 2 * GATE)), _const_spec((1, 2 * GATE)),
            _const_spec((1, GATE)), _const_spec((1, GATE)),
            _const_spec((SGU_GROUPS, CHUNK, CHUNK)), _const_spec((CHUNK, SGU_GROUPS)),
            _const_spec((GATE, D_MODEL)), _const_spec((1, D_MODEL)),
            _const_spec((1, D_MODEL)), _const_spec((1, D_MODEL)),
        ],
        out_specs=[pl.BlockSpec((tm, D_MODEL), lambda i: (i, 0)), vn_spec],
        out_shape=[jax.ShapeDtypeStruct((m, D_MODEL), F32), vn_shape],
        scratch_shapes=[pltpu.VMEM((tm, GATE), BF16), pltpu.VMEM((tm, GATE), BF16)],
        compiler_params=_params(1),
        name="gmlp_layer",
    )(x, w_in, b_in, gv, bv, ws, bs_t, w_out, b_out, g, b)


def _mlp_body(x, w1_ref, w2_ref, g_ref, b_ref, out_ref):
    xb = x.astype(BF16)
    acc = jnp.zeros_like(x)
    for j in range(D_FF // D_MODEL):
        lo, hi = j * D_MODEL, (j + 1) * D_MODEL
        h = jnp.square(jnp.maximum(_dot(xb, w1_ref[:, lo:hi]), 0.0)).astype(BF16)
        acc = acc + _dot(h, w2_ref[lo:hi, :])
    out_ref[...] = _ln(ALPHA * x + acc, g_ref[...], b_ref[...])


def _mlp_kernel(x_ref, w1_ref, w2_ref, g_ref, b_ref, out_ref):
    _mlp_body(x_ref[...], w1_ref, w2_ref, g_ref, b_ref, out_ref)


def _attn_mlp_kernel(x_ref, o_ref, wo_ref, ga_ref, ba_ref, w1_ref, w2_ref, g_ref, b_ref, out_ref):
    x = _ln(ALPHA * x_ref[...] + _dot(o_ref[...].astype(BF16), wo_ref[...]), ga_ref[...], ba_ref[...])
    _mlp_body(x, w1_ref, w2_ref, g_ref, b_ref, out_ref)


def _mlp_rounding_kernel(*refs, has_attn):
    if has_attn:
        (x_ref, o_ref, wo_ref, ga_ref, ba_ref, w1_ref, w2_ref, g_ref, b_ref,
         out_ref, w1b_ref, w2b_ref, wob_ref, x_sc, xb_sc, acc_sc) = refs
    else:
        x_ref, w1_ref, w2_ref, g_ref, b_ref, out_ref, w1b_ref, w2b_ref, x_sc, xb_sc, acc_sc = refs
    j = pl.program_id(0)

    @pl.when(j == 0)
    def _():
        x = x_ref[...]
        if has_attn:
            wob = wo_ref[...].astype(BF16)
            wob_ref[...] = wob
            x = _ln(ALPHA * x + _dot(o_ref[...].astype(BF16), wob), ga_ref[...], ba_ref[...])
        x_sc[...] = x
        xb_sc[...] = x.astype(BF16)
        acc_sc[...] = jnp.zeros_like(acc_sc)

    w1b = w1_ref[...].astype(BF16)
    w2b = w2_ref[...].astype(BF16)
    w1b_ref[...] = w1b
    w2b_ref[...] = w2b
    h = jnp.square(jnp.maximum(_dot(xb_sc[...], w1b), 0.0)).astype(BF16)
    acc_sc[...] = acc_sc[...] + _dot(h, w2b)

    @pl.when(j == pl.num_programs(0) - 1)
    def _():
        out_ref[...] = _ln(ALPHA * x_sc[...] + acc_sc[...], g_ref[...], b_ref[...])


def _mlp_layer_rounding(x, w1, w2, layer, g, b, *, attn=None):
    m = x.shape[0]
    chunk = D_MODEL
    rows = _const_spec((m, D_MODEL))
    vec = _const_spec((1, D_MODEL))
    w_specs = [pl.BlockSpec((None, D_MODEL, chunk), lambda j: (layer, 0, j)),
               pl.BlockSpec((None, chunk, D_MODEL), lambda j: (layer, j, 0)), vec, vec]
    whole = lambda *shape: pl.BlockSpec(shape, lambda j: (0,) * len(shape))
    out_specs = [whole(m, D_MODEL), pl.BlockSpec((D_MODEL, chunk), lambda j: (0, j)),
                 pl.BlockSpec((chunk, D_MODEL), lambda j: (j, 0))]
    out_shape = [jax.ShapeDtypeStruct((m, D_MODEL), F32), jax.ShapeDtypeStruct((D_MODEL, D_FF), BF16),
                 jax.ShapeDtypeStruct((D_FF, D_MODEL), BF16)]
    if attn is None:
        args, specs = (x, w1, w2, g, b), [rows] + w_specs
    else:
        o, wo, ga, ba = attn
        args = (x, o, wo, ga, ba, w1, w2, g, b)
        specs = [rows, rows, _const_spec((D_MODEL, D_MODEL)), vec, vec] + w_specs
        out_specs.append(whole(D_MODEL, D_MODEL))
        out_shape.append(jax.ShapeDtypeStruct((D_MODEL, D_MODEL), BF16))
    return pl.pallas_call(
        functools.partial(_mlp_rounding_kernel, has_attn=attn is not None),
        grid=(D_FF // chunk,), in_specs=specs, out_specs=out_specs, out_shape=out_shape,
        scratch_shapes=[pltpu.VMEM((m, D_MODEL), F32), pltpu.VMEM((m, D_MODEL), BF16), pltpu.VMEM((m, D_MODEL), F32)],
        compiler_params=_params(1), name="mlp_layer_rounding",
    )(*args)


def _mlp_layer(x, w1, w2, g, b, *, tm, attn=None):
    m = x.shape[0]
    row_spec = pl.BlockSpec((tm, D_MODEL), lambda i: (i, 0))
    vec = _const_spec((1, D_MODEL))
    mlp_specs = [_const_spec((D_MODEL, D_FF)), _const_spec((D_FF, D_MODEL)), vec, vec]
    if attn is None:
        kern, args, specs = _mlp_kernel, (x, w1, w2, g, b), [row_spec] + mlp_specs
    else:
        o, wo, ga, ba = attn
        kern = _attn_mlp_kernel
        args = (x, o, wo, ga, ba, w1, w2, g, b)
        specs = [row_spec, row_spec, _const_spec((D_MODEL, D_MODEL)), vec, vec] + mlp_specs
    return pl.pallas_call(
        kern, grid=(m // tm,), in_specs=specs, out_specs=row_spec,
        out_shape=jax.ShapeDtypeStruct((m, D_MODEL), F32),
        compiler_params=_params(1), name="mlp_layer",
    )(*args)


def _matmul_kernel(x_ref, w_ref, o_ref, xb_sc):
    @pl.when(pl.program_id(1) == 0)
    def _():
        xb_sc[...] = x_ref[...].astype(BF16)

    o_ref[...] = _dot(xb_sc[...], w_ref[...])


def _matmul(x, w, *, tm, tn):
    m, k = x.shape
    n = w.shape[1]
    return pl.pallas_call(
        _matmul_kernel, grid=(m // tm, n // tn),
        in_specs=[pl.BlockSpec((tm, k), lambda i, j: (i, 0)), pl.BlockSpec((k, tn), lambda i, j: (0, j))],
        out_specs=pl.BlockSpec((tm, tn), lambda i, j: (i, j)),
        out_shape=jax.ShapeDtypeStruct((m, n), F32),
        scratch_shapes=[pltpu.VMEM((tm, k), BF16)],
        compiler_params=_params(2), name="qkv_projection",
    )(x, w)


def _kv_state_kernel(x_ref, wt_ref, o_ref):
    o_ref[...] = _dot_nt(wt_ref[...], x_ref[...].astype(BF16))


def _kv_state(x, wkv_t, g, *, batch, seq, rows, tw):
    per_seq = seq // tw
    first = (seq - rows) // tw
    return pl.pallas_call(
        _kv_state_kernel, grid=(batch, rows // tw),
        in_specs=[pl.BlockSpec((tw, D_MODEL), lambda b, j: (b * per_seq + first + j, 0)),
                  pl.BlockSpec((None, 2 * D_MODEL, D_MODEL), lambda b, j: (g, 0, 0), pipeline_mode=pl.Buffered(1))],
        out_specs=pl.BlockSpec((None, 2 * D_MODEL, tw), lambda b, j: (b, 0, j)),
        out_shape=jax.ShapeDtypeStruct((batch, 2 * D_MODEL, rows), F32),
        compiler_params=_params(2), name="kv_state",
    )(x, wkv_t)


def _kv_state_sample(x3, wkv_t):
    n_new, nb, _ = x3.shape
    return pl.pallas_call(
        _kv_state_kernel, grid=(N_BRANCH, n_new),
        in_specs=[pl.BlockSpec((None, nb, D_MODEL), lambda g, i: (i, 0, 0)),
                  pl.BlockSpec((None, 2 * D_MODEL, D_MODEL), lambda g, i: (g, 0, 0))],
        out_specs=pl.BlockSpec((None, None, 2 * D_MODEL, nb), lambda g, i: (g, i, 0, 0)),
        out_shape=jax.ShapeDtypeStruct((N_BRANCH, n_new, 2 * D_MODEL, nb), F32),
        compiler_params=_params(2), name="kv_state_sample",
    )(x3, wkv_t)


def _attend_block(q2, ks, vs, bias, no_prev, head_a, in_prev):
    qb = (q2 * Q_SCALE).astype(BF16)
    zero = jnp.zeros_like(qb)
    qs = jnp.concatenate([jnp.where(head_a, qb, zero), jnp.where(head_a, zero, qb)], axis=0)
    s = _dot_nt(qs, ks.astype(BF16)) + bias
    if no_prev is not None:
        s = s + jnp.where(in_prev, no_prev, 0.0)
    m = jnp.max(s, axis=-1, keepdims=True)
    p = jnp.exp(s - m)
    vs_aug = jnp.concatenate([vs.astype(BF16), jnp.ones((2 * KEY_BLOCK, LANES), BF16)], axis=1)
    ov = _dot(p.astype(BF16), vs_aug)
    num2 = jnp.where(head_a, ov[:KEY_BLOCK, :LANES], ov[KEY_BLOCK:, :LANES])
    den2 = jnp.where(head_a, ov[:KEY_BLOCK, LANES:], ov[KEY_BLOCK:, LANES:])
    l2 = jnp.where(head_a, m[:KEY_BLOCK], m[KEY_BLOCK:]) + jnp.log(den2)
    return num2 / den2, l2


def _prompt_attn_kernel(*refs, max_group):
    ins, (steps_ref, step0_ref, o_ref), scr = refs[:15], refs[15:18], refs[18:]
    oscs, lscs, bias_sc = scr[0:3], scr[3:6], scr[6]
    t = pl.program_id(2)
    lane = lax.broadcasted_iota(jnp.int32, (KEY_BLOCK, LANES), 1)
    qrow = lax.broadcasted_iota(jnp.int32, (KEY_BLOCK, LANES), 0)
    head_a = lane < HEAD_DIM
    kcol = lax.broadcasted_iota(jnp.int32, (2 * KEY_BLOCK, 2 * KEY_BLOCK), 1)
    in_prev = kcol < KEY_BLOCK
    no_prev_first = jnp.where(t == 0, NEG, 0.0)

    @pl.when(jnp.logical_and(pl.program_id(1) == 0, t == 0))
    def _():
        for g in range(N_BRANCH):
            for hd in range(HEADS_PER_STEP):
                steps = jnp.broadcast_to(steps_ref[g, hd], (KEY_BLOCK, LANES))
                rolled = pltpu.roll(steps, 0, 1, stride=1, stride_axis=0)
                step0 = jnp.broadcast_to(step0_ref[g, hd], (KEY_BLOCK, LANES))
                left = jnp.where(lane >= qrow, rolled, NEG)
                right = jnp.where(lane < qrow, rolled, jnp.where(lane == qrow, step0, NEG))
                bias_sc[g, hd * KEY_BLOCK:(hd + 1) * KEY_BLOCK, :] = jnp.concatenate([left, right], axis=1)

    pooled, loops = [], []
    for g, d in enumerate(DILATIONS):
        q_ref, kc_ref, vc_ref, kp_ref, vp_ref = ins[5 * g:5 * g + 5]
        nqb = Q_TILE // d // KEY_BLOCK

        def rows(start, size, d=d):
            if d == 1:
                return pl.ds(start if isinstance(start, int) else pl.multiple_of(start, KEY_BLOCK), size)
            return pl.ds(start, size, stride=d)

        def load_first(r, q_ref=q_ref, kc_ref=kc_ref, vc_ref=vc_ref, kp_ref=kp_ref, vp_ref=vp_ref, rows=rows):
            sel = rows(r, KEY_BLOCK)
            ks = jnp.concatenate([kp_ref[sel, :], kc_ref[sel, :]], axis=0)
            vs = jnp.concatenate([vp_ref[sel, :], vc_ref[sel, :]], axis=0)
            return sel, q_ref[sel, :], ks, vs

        def load_later(sp, d=d, nqb=nqb, q_ref=q_ref, kc_ref=kc_ref, vc_ref=vc_ref, rows=rows):
            r = sp // (nqb - 1)
            qb = 1 + sp % (nqb - 1)
            q_sel = rows(d * KEY_BLOCK * qb + r, KEY_BLOCK)
            k_sel = rows(d * KEY_BLOCK * (qb - 1) + r, 2 * KEY_BLOCK)
            return q_sel, q_ref[q_sel, :], kc_ref[k_sel, :], vc_ref[k_sel, :]

        if d < max_group:
            pooled += [(g, load_first, r, no_prev_first) for r in range(d)]
        else:
            loops.append((g, d, load_first, no_prev_first))
        if nqb > 1:
            loops.append((g, d * (nqb - 1), load_later, None))

    def attend_group(items):
        loaded = [(g, no_prev) + load(idx) for g, load, idx, no_prev in items]
        done = [(g, sel, _attend_block(q2, ks, vs, bias_sc[g], no_prev, head_a, in_prev))
                for g, no_prev, sel, q2, ks, vs in loaded]
        for g, sel, (o2, l2) in done:
            oscs[g][sel, :] = o2
            lscs[g][sel, :] = l2

    for i in range(0, len(pooled), max_group):
        attend_group(pooled[i:i + max_group])
    for g, n_blocks, load, no_prev in loops:
        group = max(u for u in range(1, max_group + 1) if n_blocks % u == 0)

        def body(it, carry, g=g, load=load, no_prev=no_prev, group=group):
            attend_group([(g, load, it * group + u, no_prev) for u in range(group)])
            return carry

        lax.fori_loop(0, n_blocks // group, body, 0)

    merge_rows = 256
    for c in range(Q_TILE // merge_rows):
        rs = slice(c * merge_rows, (c + 1) * merge_rows)
        ls = [lscs[g][rs, :] for g in range(N_BRANCH)]
        m = jnp.maximum(jnp.maximum(ls[0], ls[1]), ls[2])
        es = [jnp.exp(l - m) for l in ls]
        num = es[0] * oscs[0][rs, :] + es[1] * oscs[1][rs, :] + es[2] * oscs[2][rs, :]
        o_ref[rs, :] = (num / (es[0] + es[1] + es[2])).astype(o_ref.dtype)


def _prompt_attention(qkv, steps_rev, step0, *, batch, seq, max_group=6):
    n_tiles = seq // Q_TILE
    in_specs = []
    for g, d in enumerate(DILATIONS):
        prev_rows = KEY_BLOCK * d
        per_tile = Q_TILE // prev_rows
        q_col = g * N_PAIRS
        k_col = (KV_OFFSET + 2 * g * D_MODEL) // LANES
        v_col = k_col + N_PAIRS

        def cur(col):
            return pl.BlockSpec((Q_TILE, LANES), lambda hp, b, t, col=col: (b * n_tiles + t, col + hp))

        def prev(col, per_tile=per_tile, prev_rows=prev_rows):
            return pl.BlockSpec(
                (prev_rows, LANES),
                lambda hp, b, t, col=col: (jnp.maximum((b * n_tiles + t) * per_tile - 1, 0), col + hp))

        in_specs += [cur(q_col), cur(k_col), cur(v_col), prev(k_col), prev(v_col)]
    head_vec = pl.BlockSpec((N_BRANCH, HEADS_PER_STEP, 1, LANES), lambda hp, b, t: (0, hp, 0, 0))
    in_specs += [head_vec, head_vec]
    return pl.pallas_call(
        functools.partial(_prompt_attn_kernel, max_group=max_group),
        grid=(N_PAIRS, batch, n_tiles),
        in_specs=in_specs,
        out_specs=pl.BlockSpec((Q_TILE, LANES), lambda hp, b, t: (b * n_tiles + t, hp)),
        out_shape=jax.ShapeDtypeStruct((batch * seq, D_MODEL), BF16),
        scratch_shapes=[pltpu.VMEM((Q_TILE, LANES), F32)] * (2 * N_BRANCH)
        + [pltpu.VMEM((N_BRANCH, 2 * KEY_BLOCK, 2 * KEY_BLOCK), F32)],
        compiler_params=_params(3), name="prompt_attention",
    )(*([qkv] * 15), steps_rev, step0)


def _sample_attn_kernel(*refs, n_new, heads):
    qkv_refs, caches, cbias, bn_ref, o_ref = refs[:9], refs[9:12], refs[12:15], refs[15], refs[16]
    j = pl.program_id(1)
    which = pl.program_id(2)
    width = heads * HEAD_DIM
    per_block = NEW_ROWS // n_new

    def own_rows(ref):
        rows = ref[0:n_new, :]
        for w in range(1, per_block):
            rows = jnp.where(which == w, ref[w * n_new:(w + 1) * n_new, :], rows)
        return rows

    pad = jnp.zeros((NEW_ROWS - n_new, width), F32)
    lane_head = lax.broadcasted_iota(jnp.int32, (heads, width), 1) // HEAD_DIM
    own_lanes = lane_head == lax.broadcasted_iota(jnp.int32, (heads, width), 0)

    def per_head_rows(x):
        return jnp.concatenate(
            [jnp.sum(jnp.where(own_lanes, x[i * heads:(i + 1) * heads], 0.0), axis=0, keepdims=True)
             for i in range(n_new)], axis=0)

    outs, lses = [], []
    for g in range(N_BRANCH):
        q = own_rows(qkv_refs[3 * g]) * Q_SCALE
        k_new = jnp.concatenate([own_rows(qkv_refs[3 * g + 1]), pad], axis=0).astype(BF16)
        v_new = jnp.concatenate([own_rows(qkv_refs[3 * g + 2]), pad], axis=0).astype(BF16)
        qbd = jnp.concatenate([jnp.where(own_lanes, q[i:i + 1, :], 0.0) for i in range(n_new)],
                              axis=0).astype(BF16)
        n_pos = caches[g].shape[-1]
        k_t = caches[g][0, 0].reshape(width, n_pos).astype(BF16)
        v_t = caches[g][0, 1].reshape(width, n_pos).astype(BF16)
        s_c = _dot(qbd, k_t) + cbias[g][j]
        s_n = _dot_nt(qbd, k_new) + bn_ref[g, j]
        m = jnp.maximum(jnp.max(s_c, axis=-1, keepdims=True), jnp.max(s_n, axis=-1, keepdims=True))
        p_c = jnp.exp(s_c - m)
        p_n = jnp.exp(s_n - m)
        den = jnp.sum(p_c, axis=-1, keepdims=True) + jnp.sum(p_n, axis=-1, keepdims=True)
        o = (_dot_nt(p_c.astype(BF16), v_t) + _dot(p_n.astype(BF16), v_new)) / den
        outs.append(per_head_rows(o))
        lses.append(per_head_rows(jnp.broadcast_to(m + jnp.log(den), o.shape)))
    m = jnp.maximum(jnp.maximum(lses[0], lses[1]), lses[2])
    es = [jnp.exp(l - m) for l in lses]
    num = es[0] * outs[0] + es[1] * outs[1] + es[2] * outs[2]
    merged = num / (es[0] + es[1] + es[2])
    for w in range(per_block):
        @pl.when(which == w)
        def _(w=w):
            o_ref[w * n_new:(w + 1) * n_new, :] = merged


def _sample_attention(qkv, caches, cache_bias, new_bias, *, n_new, heads_per_step=8):
    hs = heads_per_step
    width = hs * HEAD_DIM
    per_block = NEW_ROWS // n_new
    n_blocks = qkv.shape[0] // NEW_ROWS
    window = lambda col: pl.BlockSpec((NEW_ROWS, width), lambda r, j, w, col=col: (r, col + j))
    per_d = D_MODEL // width
    windows = []
    for g in range(N_BRANCH):
        k_col = (KV_OFFSET + 2 * g * D_MODEL) // width
        windows += [window(g * per_d), window(k_col), window(k_col + per_d)]
    return pl.pallas_call(
        functools.partial(_sample_attn_kernel, n_new=n_new, heads=hs),
        grid=(n_blocks, N_HEADS // hs, per_block),
        in_specs=windows
        + [pl.BlockSpec((1, 2, hs, HEAD_DIM, c.shape[-1]), lambda r, j, w: (r * per_block + w, 0, j, 0, 0))
           for c in caches]
        + [_const_spec(t.shape) for t in cache_bias] + [_const_spec(new_bias.shape)],
        out_specs=pl.BlockSpec((NEW_ROWS, width), lambda r, j, w: (r, j)),
        out_shape=jax.ShapeDtypeStruct((qkv.shape[0], D_MODEL), F32),
        compiler_params=_params(3, SAMPLE_ATTN_VMEM_LIMIT_BYTES), name="sample_attention",
    )(*([qkv] * 9), *caches, *cache_bias, new_bias)


def _t5_bucket(dist):
    max_exact = NUM_BUCKETS // 2
    n = jnp.maximum(dist, 1).astype(F32)
    large = max_exact + (jnp.log(n / max_exact) / math.log(MAX_DISTANCE / max_exact)
                         * (NUM_BUCKETS - max_exact)).astype(jnp.int32)
    large = jnp.minimum(large, NUM_BUCKETS - 1)
    return jnp.where(dist < max_exact, dist, large)


def _branch_bias(rel_bias, g):
    n_keys = WINDOWS[g] // DILATIONS[g] + 1
    dist = jnp.arange(n_keys, dtype=jnp.int32) * DILATIONS[g]
    return rel_bias[_t5_bucket(dist)][:, g * N_HEADS:(g + 1) * N_HEADS].T.astype(F32)


def _sample_bias_tables(biases, n_new, hs):
    neg = lambda *shape: jnp.full(shape, NEG, F32)
    by_step = lambda t: t.reshape(N_HEADS // hs, hs, n_new, -1).transpose(0, 2, 1, 3).reshape(
        N_HEADS // hs, n_new * hs, -1)
    cache_tabs, new_tabs = [], []
    for g, d in enumerate(DILATIONS):
        w = WINDOWS[g]
        n_steps = w // d
        b = biases[g]
        rows, new_rows = [], []
        for i in range(n_new):
            first = n_steps + i // d
            hit = b[:, min(first, n_steps):i // d:-1]
            hit = jnp.concatenate([neg(N_HEADS, n_steps - hit.shape[1]), hit], axis=1)
            grid = jnp.where((jnp.arange(d) == i % d)[None, None, :], hit[:, :, None], NEG)
            rows.append(grid.reshape(N_HEADS, w))
            cols = [b[:, (i - j) // d] if (j <= i and (i - j) % d == 0 and j < n_new) else neg(N_HEADS)
                    for j in range(NEW_ROWS)]
            new_rows.append(jnp.stack(cols, axis=1))
        cache_tabs.append(by_step(jnp.stack(rows, axis=1)))
        new_tabs.append(by_step(jnp.stack(new_rows, axis=1)))
    return cache_tabs, jnp.stack(new_tabs)


def kernel(x_prompt, x_sample, cache_kv_w128, cache_kv_w512, cache_kv_w2048, ln_g, ln_b, gm_w_in, gm_b_in,
           gm_ln_g, gm_ln_b, gm_w_s, gm_b_s, gm_w_out, gm_b_out, w_kv, attn_w_q, attn_w_o, rel_bias,
           mlp_w1, mlp_w2):
    batch, seq, _ = x_prompt.shape
    n_samp, n_new, _ = x_sample.shape
    ms = n_samp * n_new
    assert DEPTH == 2 and gm_w_in.shape[0] == 1 and attn_w_q.shape[0] == 1
    assert seq % Q_TILE == 0 and ms % CHUNK == 0 and CHUNK % n_new == 0 and NEW_ROWS % n_new == 0
    assert all(PAST_LEN >= w and w % d == 0 and w // d == KEY_BLOCK for w, d in zip(WINDOWS, DILATIONS))

    bf = lambda w: w.astype(BF16)
    vec = lambda v: v.reshape(1, -1)
    w_in, w_out = bf(gm_w_in[0]), bf(gm_w_out[0])
    wqkv = jnp.concatenate([bf(attn_w_q[0]).reshape(D_MODEL, KV_OFFSET),
                            bf(w_kv).reshape(D_MODEL, 2 * KV_OFFSET)], axis=1)
    gm_args = (vec(gm_b_in[0]), vec(gm_ln_g[0]), vec(gm_ln_b[0]))
    gm_tail = (w_out, vec(gm_b_out[0]), vec(ln_g[0, 0]), vec(ln_b[0, 0]))

    xp = x_prompt.reshape(batch * seq, D_MODEL)
    xs = x_sample.reshape(ms, D_MODEL)
    x1p, vn_p = _gmlp_layer(xp, w_in, *gm_args, gm_w_s[0], gm_b_s[0].T, *gm_tail, tm=512, rows_per_seq=seq)
    reps = CHUNK // n_new
    same_seq = (jnp.arange(CHUNK)[:, None] // n_new) == (jnp.arange(CHUNK)[None, :] // n_new)
    ws_s = jnp.where(same_seq[None], jnp.tile(gm_w_s[0][:, :n_new, :n_new], (1, reps, reps)), 0.0)
    bs_s = jnp.tile(gm_b_s[0][:, :n_new], (1, reps)).T
    x1s, vn_s = _gmlp_layer(xs, w_in, *gm_args, ws_s, bs_s, *gm_tail, tm=ms, rows_per_seq=None)

    x2s, w1b, w2b = _mlp_layer_rounding(x1s, mlp_w1, mlp_w2, 0, vec(ln_g[0, 1]), vec(ln_b[0, 1]))
    x2p = _mlp_layer(x1p, w1b, w2b, vec(ln_g[0, 1]), vec(ln_b[0, 1]), tm=512)

    qkv_p = _matmul(x2p, wqkv, tm=1024, tn=2304)
    qkv_s = _matmul(x2s, wqkv, tm=ms, tn=1536)

    biases = [_branch_bias(rel_bias, g) for g in range(N_BRANCH)]
    steps_rev = jnp.stack([b[:, :0:-1] for b in biases])[:, :, None, :]
    step0 = jnp.stack([jnp.broadcast_to(b[:, :1], (N_HEADS, LANES)) for b in biases])[:, :, None, :]
    o_p = _prompt_attention(qkv_p, steps_rev, step0, batch=batch, seq=seq)

    caches = [jnp.transpose(c, (0, 2, 3, 4, 1)) for c in (cache_kv_w128, cache_kv_w512, cache_kv_w2048)]
    heads_per_step = N_HEADS
    cache_bias, new_bias = _sample_bias_tables(biases, n_new, heads_per_step)
    o_s = _sample_attention(qkv_s, caches, cache_bias, new_bias, n_new=n_new, heads_per_step=heads_per_step)

    attn_ln = (vec(ln_g[1, 0]), vec(ln_b[1, 0]))
    y_s, w1b, w2b, wob = _mlp_layer_rounding(x2s, mlp_w1, mlp_w2, 1, vec(ln_g[1, 1]), vec(ln_b[1, 1]),
                                             attn=(o_s, attn_w_o[0]) + attn_ln)
    y_p = _mlp_layer(x2p, w1b, w2b, vec(ln_g[1, 1]), vec(ln_b[1, 1]), tm=512, attn=(o_p, wob) + attn_ln)

    wkv_t = jnp.transpose(bf(w_kv), (1, 2, 3, 4, 0)).reshape(N_BRANCH, 2 * D_MODEL, D_MODEL)
    kv_p_out = []
    for g in range(N_BRANCH):
        rows = min(WINDOWS[g], seq)
        kv_t = _kv_state(x2p, wkv_t, g, batch=batch, seq=seq, rows=rows, tw=min(rows, 512))
        kv_p_out.append(kv_t.reshape(batch, 2, N_HEADS, HEAD_DIM, rows).transpose(0, 4, 1, 2, 3))
    kv_s_t = _kv_state_sample(x2s.reshape(n_samp, n_new, D_MODEL).transpose(1, 0, 2), wkv_t)
    kv_s_out = [kv_s_t[g].reshape(n_new, 2, N_HEADS, HEAD_DIM, n_samp).transpose(4, 0, 1, 2, 3)
                for g in range(N_BRANCH)]
    return (y_p.reshape(batch, seq, D_MODEL), y_s.reshape(n_samp, n_new, D_MODEL),
            vn_p[None], vn_s.reshape(1, n_samp, n_new, GATE),
            kv_p_out[0], kv_s_out[0], kv_p_out[1], kv_s_out[1], kv_p_out[2], kv_s_out[2])
```

```python
import functools
import math

import jax
import jax.numpy as jnp
from jax import lax
from jax.experimental import pallas as pl
from jax.experimental.pallas import tpu as pltpu

D_MODEL = 1024
CHUNK = 128
GATE = 2 * D_MODEL
SGU_GROUPS = 8
SGU_GROUP_DIM = GATE // SGU_GROUPS
WINDOWS = (128, 512, 2048)
DILATIONS = (1, 4, 16)
N_BRANCH = 3
HEAD_DIM = 64
N_HEADS = 16
D_FF = 4 * D_MODEL
NUM_BUCKETS = 32
MAX_DISTANCE = 2048
PAST_LEN = 2048
DEPTH = 2
ALPHA = (2 * DEPTH) ** 0.25
LN_EPS = 1e-5
NEG = -1e30
Q_SCALE = HEAD_DIM ** -0.5

BF16 = jnp.bfloat16
F32 = jnp.float32

VMEM_LIMIT_BYTES = 56 * 1024 * 1024
SAMPLE_ATTN_VMEM_LIMIT_BYTES = 60 * 1024 * 1024
LANES = 128
Q_TILE = 2048
KEY_BLOCK = 128
HEADS_PER_STEP = LANES // HEAD_DIM
N_PAIRS = N_HEADS // HEADS_PER_STEP
NEW_ROWS = 8


def _params(n_axes, vmem_limit_bytes=VMEM_LIMIT_BYTES):
    return pltpu.CompilerParams(dimension_semantics=("arbitrary",) * n_axes, vmem_limit_bytes=vmem_limit_bytes)


def _const_spec(shape):
    return pl.BlockSpec(shape, lambda *_: (0,) * len(shape), pipeline_mode=pl.Buffered(1))


def _ln(x, g, b):
    mu = jnp.mean(x, axis=-1, keepdims=True)
    xc = x - mu
    var = jnp.mean(xc * xc, axis=-1, keepdims=True)
    return xc * lax.rsqrt(var + LN_EPS) * g + b


def _dot(a, b):
    return jnp.dot(a, b, preferred_element_type=F32)


def _dot_nt(a, b):
    return lax.dot_general(a, b, (((1,), (1,)), ((), ())), preferred_element_type=F32)


def _gmlp_kernel(x_ref, w_in_ref, b_in_ref, gv_ref, bv_ref, ws_ref, bs_ref, w_out_ref, b_out_ref,
                 g_ref, b_ref, x1_ref, vn_ref, vnb_sc, gact_sc, *, tm, blocks_per_seq):
    x = x_ref[...]
    xb = x.astype(BF16)
    zv = _dot(xb, w_in_ref[:, GATE:]) + b_in_ref[:, GATE:]
    vn = _ln(jax.nn.gelu(zv), gv_ref[...], bv_ref[...])
    vnb_sc[...] = vn.astype(BF16)

    if blocks_per_seq is None:
        vn_ref[...] = vn
    else:
        @pl.when(pl.program_id(0) % blocks_per_seq == blocks_per_seq - 1)
        def _():
            vn_ref[0] = vn[tm - CHUNK:, :]

    row = lax.broadcasted_iota(jnp.int32, (CHUNK, CHUNK), 0)
    col = lax.broadcasted_iota(jnp.int32, (CHUNK, CHUNK), 1)
    causal = row >= col
    for g in range(SGU_GROUPS):
        lo, hi = g * SGU_GROUP_DIM, (g + 1) * SGU_GROUP_DIM
        zu = jax.nn.gelu(_dot(xb, w_in_ref[:, lo:hi]) + b_in_ref[:, lo:hi])
        wg = jnp.where(causal, ws_ref[g], 0.0).astype(BF16)
        bsg = bs_ref[:, g:g + 1]
        for c in range(tm // CHUNK):
            r0, r1 = c * CHUNK, (c + 1) * CHUNK
            s = _dot(wg, vnb_sc[r0:r1, lo:hi]) + bsg
            gact_sc[r0:r1, lo:hi] = (zu[r0:r1] * s).astype(BF16)
    y = _dot(gact_sc[...], w_out_ref[...]) + b_out_ref[...]
    x1_ref[...] = _ln(ALPHA * x + y, g_ref[...], b_ref[...])


def _gmlp_layer(x, w_in, b_in, gv, bv, ws, bs_t, w_out, b_out, g, b, *, tm, rows_per_seq):
    m = x.shape[0]
    if rows_per_seq is None:
        blocks_per_seq = None
        vn_shape = jax.ShapeDtypeStruct((m, GATE), F32)
        vn_spec = pl.BlockSpec((tm, GATE), lambda i: (i, 0))
    else:
        blocks_per_seq = rows_per_seq // tm
        vn_shape = jax.ShapeDtypeStruct((m // rows_per_seq, CHUNK, GATE), F32)
        vn_spec = pl.BlockSpec((1, CHUNK, GATE), lambda i: (i // blocks_per_seq, 0, 0))
    return pl.pallas_call(
        functools.partial(_gmlp_kernel, tm=tm, blocks_per_seq=blocks_per_seq),
        grid=(m // tm,),
        in_specs=[
            pl.BlockSpec((tm, D_MODEL), lambda i: (i, 0)),
            _const_spec((D_MODEL, 2 * GATE)), _const_spec((1, 2 * GATE)),
            _const_spec((1, GATE)), _const_spec((1, GATE)),
            _const_spec((SGU_GROUPS, CHUNK, CHUNK)), _const_spec((CHUNK, SGU_GROUPS)),
            _const_spec((GATE, D_MODEL)), _const_spec((1, D_MODEL)),
            _const_spec((1, D_MODEL)), _const_spec((1, D_MODEL)),
        ],
        out_specs=[pl.BlockSpec((tm, D_MODEL), lambda i: (i, 0)), vn_spec],
        out_shape=[jax.ShapeDtypeStruct((m, D_MODEL), F32), vn_shape],
        scratch_shapes=[pltpu.VMEM((tm, GATE), BF16), pltpu.VMEM((tm, GATE), BF16)],
        compiler_params=_params(1),
        name="gmlp_layer",
    )(x, w_in, b_in, gv, bv, ws, bs_t, w_out, b_out, g, b)


def _mlp_body(x, w1_ref, w2_ref, g_ref, b_ref, out_ref):
    xb = x.astype(BF16)
    acc = jnp.zeros_like(x)
    for j in range(D_FF // D_MODEL):
        lo, hi = j * D_MODEL, (j + 1) * D_MODEL
        h = jnp.square(jnp.maximum(_dot(xb, w1_ref[:, lo:hi]), 0.0)).astype(BF16)
        acc = acc + _dot(h, w2_ref[lo:hi, :])
    out_ref[...] = _ln(ALPHA * x + acc, g_ref[...], b_ref[...])


def _mlp_kernel(x_ref, w1_ref, w2_ref, g_ref, b_ref, out_ref):
    _mlp_body(x_ref[...], w1_ref, w2_ref, g_ref, b_ref, out_ref)


def _attn_mlp_kernel(x_ref, o_ref, wo_ref, ga_ref, ba_ref, w1_ref, w2_ref, g_ref, b_ref, out_ref):
    x = _ln(ALPHA * x_ref[...] + _dot(o_ref[...].astype(BF16), wo_ref[...]), ga_ref[...], ba_ref[...])
    _mlp_body(x, w1_ref, w2_ref, g_ref, b_ref, out_ref)


def _mlp_rounding_kernel(*refs, has_attn):
    if has_attn:
        (x_ref, o_ref, wo_ref, ga_ref, ba_ref, w1_ref, w2_ref, g_ref, b_ref,
         out_ref, w1b_ref, w2b_ref, wob_ref, x_sc, xb_sc, acc_sc) = refs
    else:
        x_ref, w1_ref, w2_ref, g_ref, b_ref, out_ref, w1b_ref, w2b_ref, x_sc, xb_sc, acc_sc = refs
    j = pl.program_id(0)

    @pl.when(j == 0)
    def _():
        x = x_ref[...]
        if has_attn:
            wob = wo_ref[...].astype(BF16)
            wob_ref[...] = wob
            x = _ln(ALPHA * x + _dot(o_ref[...].astype(BF16), wob), ga_ref[...], ba_ref[...])
        x_sc[...] = x
        xb_sc[...] = x.astype(BF16)
        acc_sc[...] = jnp.zeros_like(acc_sc)

    w1b = w1_ref[...].astype(BF16)
    w2b = w2_ref[...].astype(BF16)
    w1b_ref[...] = w1b
    w2b_ref[...] = w2b
    h = jnp.square(jnp.maximum(_dot(xb_sc[...], w1b), 0.0)).astype(BF16)
    acc_sc[...] = acc_sc[...] + _dot(h, w2b)

    @pl.when(j == pl.num_programs(0) - 1)
    def _():
        out_ref[...] = _ln(ALPHA * x_sc[...] + acc_sc[...], g_ref[...], b_ref[...])


def _mlp_layer_rounding(x, w1, w2, layer, g, b, *, attn=None):
    m = x.shape[0]
    chunk = D_MODEL
    rows = _const_spec((m, D_MODEL))
    vec = _const_spec((1, D_MODEL))
    w_specs = [pl.BlockSpec((None, D_MODEL, chunk), lambda j: (layer, 0, j)),
               pl.BlockSpec((None, chunk, D_MODEL), lambda j: (layer, j, 0)), vec, vec]
    whole = lambda *shape: pl.BlockSpec(shape, lambda j: (0,) * len(shape))
    out_specs = [whole(m, D_MODEL), pl.BlockSpec((D_MODEL, chunk), lambda j: (0, j)),
                 pl.BlockSpec((chunk, D_MODEL), lambda j: (j, 0))]
    out_shape = [jax.ShapeDtypeStruct((m, D_MODEL), F32), jax.ShapeDtypeStruct((D_MODEL, D_FF), BF16),
                 jax.ShapeDtypeStruct((D_FF, D_MODEL), BF16)]
    if attn is None:
        args, specs = (x, w1, w2, g, b), [rows] + w_specs
    else:
        o, wo, ga, ba = attn
        args = (x, o, wo, ga, ba, w1, w2, g, b)
        specs = [rows, rows, _const_spec((D_MODEL, D_MODEL)), vec, vec] + w_specs
        out_specs.append(whole(D_MODEL, D_MODEL))
        out_shape.append(jax.ShapeDtypeStruct((D_MODEL, D_MODEL), BF16))
    return pl.pallas_call(
        functools.partial(_mlp_rounding_kernel, has_attn=attn is not None),
        grid=(D_FF // chunk,), in_specs=specs, out_specs=out_specs, out_shape=out_shape,
        scratch_shapes=[pltpu.VMEM((m, D_MODEL), F32), pltpu.VMEM((m, D_MODEL), BF16), pltpu.VMEM((m, D_MODEL), F32)],
        compiler_params=_params(1), name="mlp_layer_rounding",
    )(*args)


def _mlp_layer(x, w1, w2, g, b, *, tm, attn=None):
    m = x.shape[0]
    row_spec = pl.BlockSpec((tm, D_MODEL), lambda i: (i, 0))
    vec = _const_spec((1, D_MODEL))
    mlp_specs = [_const_spec((D_MODEL, D_FF)), _const_spec((D_FF, D_MODEL)), vec, vec]
    if attn is None:
        kern, args, specs = _mlp_kernel, (x, w1, w2, g, b), [row_spec] + mlp_specs
    else:
        o, wo, ga, ba = attn
        kern = _attn_mlp_kernel
        args = (x, o, wo, ga, ba, w1, w2, g, b)
        specs = [row_spec, row_spec, _const_spec((D_MODEL, D_MODEL)), vec, vec] + mlp_specs
    return pl.pallas_call(
        kern, grid=(m // tm,), in_specs=specs, out_specs=row_spec,
        out_shape=jax.ShapeDtypeStruct((m, D_MODEL), F32),
        compiler_params=_params(1), name="mlp_layer",
    )(*args)


def _project_kernel(x_ref, wt_ref, o_ref, xb_sc):
    @pl.when(pl.program_id(1) == 0)
    def _():
        xb_sc[...] = x_ref[...].astype(BF16)

    o_ref[...] = _dot_nt(xb_sc[...], wt_ref[...])


def _project(x, w_t, *, tm, tn):
    m, k = x.shape
    n = w_t.shape[0]
    return pl.pallas_call(
        _project_kernel, grid=(m // tm, n // tn),
        in_specs=[pl.BlockSpec((tm, k), lambda i, j: (i, 0)), pl.BlockSpec((tn, k), lambda i, j: (j, 0))],
        out_specs=pl.BlockSpec((tm, tn), lambda i, j: (i, j)),
        out_shape=jax.ShapeDtypeStruct((m, n), F32),
        scratch_shapes=[pltpu.VMEM((tm, k), BF16)],
        compiler_params=_params(2), name="qkv_projection",
    )(x, w_t)


def _project_rounding_kernel(x_ref, wt_ref, o_ref, wtb_ref):
    wtb = wt_ref[...].astype(BF16)
    wtb_ref[...] = wtb
    o_ref[...] = _dot_nt(x_ref[...].astype(BF16), wtb)


def _project_rounding(x, w_t, *, tn):
    m, k = x.shape
    n = w_t.shape[0]
    return pl.pallas_call(
        _project_rounding_kernel, grid=(n // tn,),
        in_specs=[_const_spec((m, k)), pl.BlockSpec((tn, k), lambda j: (j, 0))],
        out_specs=[pl.BlockSpec((m, tn), lambda j: (0, j)), pl.BlockSpec((tn, k), lambda j: (j, 0))],
        out_shape=[jax.ShapeDtypeStruct((m, n), F32), jax.ShapeDtypeStruct((n, k), BF16)],
        compiler_params=_params(1), name="qkv_projection_rounding",
    )(x, w_t)


def _kv_state_kernel(x_ref, wt_ref, o_ref):
    o_ref[...] = _dot_nt(wt_ref[...], x_ref[...].astype(BF16))


def _kv_state(x, wkv_t, g, *, batch, seq, rows, tw):
    per_seq = seq // tw
    first = (seq - rows) // tw
    return pl.pallas_call(
        _kv_state_kernel, grid=(batch, rows // tw),
        in_specs=[pl.BlockSpec((tw, D_MODEL), lambda b, j: (b * per_seq + first + j, 0)),
                  pl.BlockSpec((None, 2 * D_MODEL, D_MODEL), lambda b, j: (g, 0, 0), pipeline_mode=pl.Buffered(1))],
        out_specs=pl.BlockSpec((None, 2 * D_MODEL, tw), lambda b, j: (b, 0, j)),
        out_shape=jax.ShapeDtypeStruct((batch, 2 * D_MODEL, rows), F32),
        compiler_params=_params(2), name="kv_state",
    )(x, wkv_t)


def _kv_state_sample(x3, wkv_t):
    n_new, nb, _ = x3.shape
    return pl.pallas_call(
        _kv_state_kernel, grid=(N_BRANCH, n_new),
        in_specs=[pl.BlockSpec((None, nb, D_MODEL), lambda g, i: (i, 0, 0)),
                  pl.BlockSpec((None, 2 * D_MODEL, D_MODEL), lambda g, i: (g, 0, 0))],
        out_specs=pl.BlockSpec((None, None, 2 * D_MODEL, nb), lambda g, i: (g, i, 0, 0)),
        out_shape=jax.ShapeDtypeStruct((N_BRANCH, n_new, 2 * D_MODEL, nb), F32),
        compiler_params=_params(2), name="kv_state_sample",
    )(x3, wkv_t)


def _attend_block(q2, ks, vs, bias, no_prev, head_a, in_prev):
    qb = (q2 * Q_SCALE).astype(BF16)
    zero = jnp.zeros_like(qb)
    qs = jnp.concatenate([jnp.where(head_a, qb, zero), jnp.where(head_a, zero, qb)], axis=0)
    s = _dot_nt(qs, ks.astype(BF16)) + bias
    if no_prev is not None:
        s = s + jnp.where(in_prev, no_prev, 0.0)
    m = jnp.max(s, axis=-1, keepdims=True)
    p = jnp.exp(s - m)
    vs_aug = jnp.concatenate([vs.astype(BF16), jnp.ones((2 * KEY_BLOCK, LANES), BF16)], axis=1)
    ov = _dot(p.astype(BF16), vs_aug)
    num2 = jnp.where(head_a, ov[:KEY_BLOCK, :LANES], ov[KEY_BLOCK:, :LANES])
    den2 = jnp.where(head_a, ov[:KEY_BLOCK, LANES:], ov[KEY_BLOCK:, LANES:])
    l2 = jnp.where(head_a, m[:KEY_BLOCK], m[KEY_BLOCK:]) + jnp.log(den2)
    return num2 / den2, l2


def _prompt_attn_kernel(*refs, max_group):
    ins, (steps_ref, step0_ref, o_ref), scr = refs[:15], refs[15:18], refs[18:]
    oscs, lscs, bias_sc = scr[0:3], scr[3:6], scr[6]
    t = pl.program_id(2)
    lane = lax.broadcasted_iota(jnp.int32, (KEY_BLOCK, LANES), 1)
    qrow = lax.broadcasted_iota(jnp.int32, (KEY_BLOCK, LANES), 0)
    head_a = lane < HEAD_DIM
    kcol = lax.broadcasted_iota(jnp.int32, (2 * KEY_BLOCK, 2 * KEY_BLOCK), 1)
    in_prev = kcol < KEY_BLOCK
    no_prev_first = jnp.where(t == 0, NEG, 0.0)

    @pl.when(jnp.logical_and(pl.program_id(1) == 0, t == 0))
    def _():
        for g in range(N_BRANCH):
            for hd in range(HEADS_PER_STEP):
                steps = jnp.broadcast_to(steps_ref[g, hd], (KEY_BLOCK, LANES))
                rolled = pltpu.roll(steps, 0, 1, stride=1, stride_axis=0)
                step0 = jnp.broadcast_to(step0_ref[g, hd], (KEY_BLOCK, LANES))
                left = jnp.where(lane >= qrow, rolled, NEG)
                right = jnp.where(lane < qrow, rolled, jnp.where(lane == qrow, step0, NEG))
                bias_sc[g, hd * KEY_BLOCK:(hd + 1) * KEY_BLOCK, :] = jnp.concatenate([left, right], axis=1)

    pooled, loops = [], []
    for g, d in enumerate(DILATIONS):
        q_ref, kc_ref, vc_ref, kp_ref, vp_ref = ins[5 * g:5 * g + 5]
        nqb = Q_TILE // d // KEY_BLOCK

        def rows(start, size, d=d):
            if d == 1:
                return pl.ds(start if isinstance(start, int) else pl.multiple_of(start, KEY_BLOCK), size)
            return pl.ds(start, size, stride=d)

        def load_first(r, q_ref=q_ref, kc_ref=kc_ref, vc_ref=vc_ref, kp_ref=kp_ref, vp_ref=vp_ref, rows=rows):
            sel = rows(r, KEY_BLOCK)
            ks = jnp.concatenate([kp_ref[sel, :], kc_ref[sel, :]], axis=0)
            vs = jnp.concatenate([vp_ref[sel, :], vc_ref[sel, :]], axis=0)
            return sel, q_ref[sel, :], ks, vs

        def load_later(sp, d=d, nqb=nqb, q_ref=q_ref, kc_ref=kc_ref, vc_ref=vc_ref, rows=rows):
            r = sp // (nqb - 1)
            qb = 1 + sp % (nqb - 1)
            q_sel = rows(d * KEY_BLOCK * qb + r, KEY_BLOCK)
            k_sel = rows(d * KEY_BLOCK * (qb - 1) + r, 2 * KEY_BLOCK)
            return q_sel, q_ref[q_sel, :], kc_ref[k_sel, :], vc_ref[k_sel, :]

        if d < max_group:
            pooled += [(g, load_first, r, no_prev_first) for r in range(d)]
        else:
            loops.append((g, d, load_first, no_prev_first))
        if nqb > 1:
            loops.append((g, d * (nqb - 1), load_later, None))

    def attend_group(items):
        loaded = [(g, no_prev) + load(idx) for g, load, idx, no_prev in items]
        done = [(g, sel, _attend_block(q2, ks, vs, bias_sc[g], no_prev, head_a, in_prev))
                for g, no_prev, sel, q2, ks, vs in loaded]
        for g, sel, (o2, l2) in done:
            oscs[g][sel, :] = o2
            lscs[g][sel, :] = l2

    for i in range(0, len(pooled), max_group):
        attend_group(pooled[i:i + max_group])
    for g, n_blocks, load, no_prev in loops:
        group = max(u for u in range(1, max_group + 1) if n_blocks % u == 0)

        def body(it, carry, g=g, load=load, no_prev=no_prev, group=group):
            attend_group([(g, load, it * group + u, no_prev) for u in range(group)])
            return carry

        lax.fori_loop(0, n_blocks // group, body, 0)

    merge_rows = 256
    for c in range(Q_TILE // merge_rows):
        rs = slice(c * merge_rows, (c + 1) * merge_rows)
        ls = [lscs[g][rs, :] for g in range(N_BRANCH)]
        m = jnp.maximum(jnp.maximum(ls[0], ls[1]), ls[2])
        es = [jnp.exp(l - m) for l in ls]
        num = es[0] * oscs[0][rs, :] + es[1] * oscs[1][rs, :] + es[2] * oscs[2][rs, :]
        o_ref[rs, :] = (num / (es[0] + es[1] + es[2])).astype(o_ref.dtype)


def _prompt_attention(q, kv, steps_rev, step0, *, batch, seq, max_group=6):
    n_tiles = seq // Q_TILE
    in_specs = []
    for g, d in enumerate(DILATIONS):
        prev_rows = KEY_BLOCK * d
        per_tile = Q_TILE // prev_rows
        q_col = g * N_PAIRS
        k_col = 2 * g * N_PAIRS
        v_col = k_col + N_PAIRS

        def cur(col):
            return pl.BlockSpec((Q_TILE, LANES), lambda hp, b, t, col=col: (b * n_tiles + t, col + hp))

        def prev(col, per_tile=per_tile, prev_rows=prev_rows):
            return pl.BlockSpec(
                (prev_rows, LANES),
                lambda hp, b, t, col=col: (jnp.maximum((b * n_tiles + t) * per_tile - 1, 0), col + hp))

        in_specs += [cur(q_col), cur(k_col), cur(v_col), prev(k_col), prev(v_col)]
    head_vec = pl.BlockSpec((N_BRANCH, HEADS_PER_STEP, 1, LANES), lambda hp, b, t: (0, hp, 0, 0))
    in_specs += [head_vec, head_vec]
    return pl.pallas_call(
        functools.partial(_prompt_attn_kernel, max_group=max_group),
        grid=(N_PAIRS, batch, n_tiles),
        in_specs=in_specs,
        out_specs=pl.BlockSpec((Q_TILE, LANES), lambda hp, b, t: (b * n_tiles + t, hp)),
        out_shape=jax.ShapeDtypeStruct((batch * seq, D_MODEL), BF16),
        scratch_shapes=[pltpu.VMEM((Q_TILE, LANES), F32)] * (2 * N_BRANCH)
        + [pltpu.VMEM((N_BRANCH, 2 * KEY_BLOCK, 2 * KEY_BLOCK), F32)],
        compiler_params=_params(3), name="prompt_attention",
    )(*([q, kv, kv, kv, kv] * N_BRANCH), steps_rev, step0)


def _sample_attn_kernel(*refs, n_new, heads):
    qkv_refs, caches, cbias, bn_ref, o_ref = refs[:9], refs[9:12], refs[12:15], refs[15], refs[16]
    j = pl.program_id(1)
    which = pl.program_id(2)
    width = heads * HEAD_DIM
    per_block = NEW_ROWS // n_new

    def own_rows(ref):
        rows = ref[0:n_new, :]
        for w in range(1, per_block):
            rows = jnp.where(which == w, ref[w * n_new:(w + 1) * n_new, :], rows)
        return rows

    pad = jnp.zeros((NEW_ROWS - n_new, width), F32)
    lane_head = lax.broadcasted_iota(jnp.int32, (heads, width), 1) // HEAD_DIM
    own_lanes = lane_head == lax.broadcasted_iota(jnp.int32, (heads, width), 0)

    def per_head_rows(x):
        return jnp.concatenate(
            [jnp.sum(jnp.where(own_lanes, x[i * heads:(i + 1) * heads], 0.0), axis=0, keepdims=True)
             for i in range(n_new)], axis=0)

    outs, lses = [], []
    for g in range(N_BRANCH):
        q = own_rows(qkv_refs[3 * g]) * Q_SCALE
        k_new = jnp.concatenate([own_rows(qkv_refs[3 * g + 1]), pad], axis=0).astype(BF16)
        v_new = jnp.concatenate([own_rows(qkv_refs[3 * g + 2]), pad], axis=0).astype(BF16)
        qbd = jnp.concatenate([jnp.where(own_lanes, q[i:i + 1, :], 0.0) for i in range(n_new)],
                              axis=0).astype(BF16)
        n_pos = caches[g].shape[-1]
        k_t = caches[g][0, 0].reshape(width, n_pos).astype(BF16)
        v_t = caches[g][0, 1].reshape(width, n_pos).astype(BF16)
        s_c = _dot(qbd, k_t) + cbias[g][j]
        s_n = _dot_nt(qbd, k_new) + bn_ref[g, j]
        m = jnp.maximum(jnp.max(s_c, axis=-1, keepdims=True), jnp.max(s_n, axis=-1, keepdims=True))
        p_c = jnp.exp(s_c - m)
        p_n = jnp.exp(s_n - m)
        den = jnp.sum(p_c, axis=-1, keepdims=True) + jnp.sum(p_n, axis=-1, keepdims=True)
        o = (_dot_nt(p_c.astype(BF16), v_t) + _dot(p_n.astype(BF16), v_new)) / den
        outs.append(per_head_rows(o))
        lses.append(per_head_rows(jnp.broadcast_to(m + jnp.log(den), o.shape)))
    m = jnp.maximum(jnp.maximum(lses[0], lses[1]), lses[2])
    es = [jnp.exp(l - m) for l in lses]
    num = es[0] * outs[0] + es[1] * outs[1] + es[2] * outs[2]
    merged = num / (es[0] + es[1] + es[2])
    for w in range(per_block):
        @pl.when(which == w)
        def _(w=w):
            o_ref[w * n_new:(w + 1) * n_new, :] = merged


def _sample_attention(q, kv, caches, cache_bias, new_bias, *, n_new, heads_per_step=8):
    hs = heads_per_step
    width = hs * HEAD_DIM
    per_block = NEW_ROWS // n_new
    n_blocks = q.shape[0] // NEW_ROWS
    window = lambda col: pl.BlockSpec((NEW_ROWS, width), lambda r, j, w, col=col: (r, col + j))
    per_d = D_MODEL // width
    windows = []
    for g in range(N_BRANCH):
        windows += [window(g * per_d), window(2 * g * per_d), window((2 * g + 1) * per_d)]
    return pl.pallas_call(
        functools.partial(_sample_attn_kernel, n_new=n_new, heads=hs),
        grid=(n_blocks, N_HEADS // hs, per_block),
        in_specs=windows
        + [pl.BlockSpec((1, 2, hs, HEAD_DIM, c.shape[-1]), lambda r, j, w: (r * per_block + w, 0, j, 0, 0))
           for c in caches]
        + [_const_spec(t.shape) for t in cache_bias] + [_const_spec(new_bias.shape)],
        out_specs=pl.BlockSpec((NEW_ROWS, width), lambda r, j, w: (r, j)),
        out_shape=jax.ShapeDtypeStruct((q.shape[0], D_MODEL), F32),
        compiler_params=_params(3, SAMPLE_ATTN_VMEM_LIMIT_BYTES), name="sample_attention",
    )(*([q, kv, kv] * N_BRANCH), *caches, *cache_bias, new_bias)


def _t5_bucket(dist):
    max_exact = NUM_BUCKETS // 2
    n = jnp.maximum(dist, 1).astype(F32)
    large = max_exact + (jnp.log(n / max_exact) / math.log(MAX_DISTANCE / max_exact)
                         * (NUM_BUCKETS - max_exact)).astype(jnp.int32)
    large = jnp.minimum(large, NUM_BUCKETS - 1)
    return jnp.where(dist < max_exact, dist, large)


def _branch_bias(rel_bias, g):
    n_keys = WINDOWS[g] // DILATIONS[g] + 1
    dist = jnp.arange(n_keys, dtype=jnp.int32) * DILATIONS[g]
    return rel_bias[_t5_bucket(dist)][:, g * N_HEADS:(g + 1) * N_HEADS].T.astype(F32)


def _sample_bias_tables(biases, n_new, hs):
    neg = lambda *shape: jnp.full(shape, NEG, F32)
    by_step = lambda t: t.reshape(N_HEADS // hs, hs, n_new, -1).transpose(0, 2, 1, 3).reshape(
        N_HEADS // hs, n_new * hs, -1)
    cache_tabs, new_tabs = [], []
    for g, d in enumerate(DILATIONS):
        w = WINDOWS[g]
        n_steps = w // d
        b = biases[g]
        rows, new_rows = [], []
        for i in range(n_new):
            first = n_steps + i // d
            hit = b[:, min(first, n_steps):i // d:-1]
            hit = jnp.concatenate([neg(N_HEADS, n_steps - hit.shape[1]), hit], axis=1)
            grid = jnp.where((jnp.arange(d) == i % d)[None, None, :], hit[:, :, None], NEG)
            rows.append(grid.reshape(N_HEADS, w))
            cols = [b[:, (i - j) // d] if (j <= i and (i - j) % d == 0 and j < n_new) else neg(N_HEADS)
                    for j in range(NEW_ROWS)]
            new_rows.append(jnp.stack(cols, axis=1))
        cache_tabs.append(by_step(jnp.stack(rows, axis=1)))
        new_tabs.append(by_step(jnp.stack(new_rows, axis=1)))
    return cache_tabs, jnp.stack(new_tabs)


def kernel(x_prompt, x_sample, cache_kv_w128, cache_kv_w512, cache_kv_w2048, ln_g, ln_b, gm_w_in, gm_b_in,
           gm_ln_g, gm_ln_b, gm_w_s, gm_b_s, gm_w_out, gm_b_out, w_kv, attn_w_q, attn_w_o, rel_bias,
           mlp_w1, mlp_w2):
    batch, seq, _ = x_prompt.shape
    n_samp, n_new, _ = x_sample.shape
    ms = n_samp * n_new
    assert DEPTH == 2 and gm_w_in.shape[0] == 1 and attn_w_q.shape[0] == 1
    assert seq % Q_TILE == 0 and ms % CHUNK == 0 and CHUNK % n_new == 0 and NEW_ROWS % n_new == 0
    assert all(PAST_LEN >= w and w % d == 0 and w // d == KEY_BLOCK for w, d in zip(WINDOWS, DILATIONS))

    bf = lambda w: w.astype(BF16)
    vec = lambda v: v.reshape(1, -1)
    w_in, w_out = bf(gm_w_in[0]), bf(gm_w_out[0])
    wq_t = jnp.transpose(attn_w_q[0], (1, 2, 3, 0)).reshape(N_BRANCH * D_MODEL, D_MODEL)
    wkv_t = jnp.transpose(w_kv, (1, 2, 3, 4, 0)).reshape(N_BRANCH * 2 * D_MODEL, D_MODEL)
    gm_args = (vec(gm_b_in[0]), vec(gm_ln_g[0]), vec(gm_ln_b[0]))
    gm_tail = (w_out, vec(gm_b_out[0]), vec(ln_g[0, 0]), vec(ln_b[0, 0]))

    xp = x_prompt.reshape(batch * seq, D_MODEL)
    xs = x_sample.reshape(ms, D_MODEL)
    x1p, vn_p = _gmlp_layer(xp, w_in, *gm_args, gm_w_s[0], gm_b_s[0].T, *gm_tail, tm=512, rows_per_seq=seq)
    reps = CHUNK // n_new
    same_seq = (jnp.arange(CHUNK)[:, None] // n_new) == (jnp.arange(CHUNK)[None, :] // n_new)
    ws_s = jnp.where(same_seq[None], jnp.tile(gm_w_s[0][:, :n_new, :n_new], (1, reps, reps)), 0.0)
    bs_s = jnp.tile(gm_b_s[0][:, :n_new], (1, reps)).T
    x1s, vn_s = _gmlp_layer(xs, w_in, *gm_args, ws_s, bs_s, *gm_tail, tm=ms, rows_per_seq=None)

    x2s, w1b, w2b = _mlp_layer_rounding(x1s, mlp_w1, mlp_w2, 0, vec(ln_g[0, 1]), vec(ln_b[0, 1]))
    x2p = _mlp_layer(x1p, w1b, w2b, vec(ln_g[0, 1]), vec(ln_b[0, 1]), tm=512)

    q_s, wq_tb = _project_rounding(x2s, wq_t, tn=D_MODEL)
    kv_s, wkv_tb = _project_rounding(x2s, wkv_t, tn=D_MODEL)
    q_p = _project(x2p, wq_tb, tm=1024, tn=1536)
    kv_p = _project(x2p, wkv_tb, tm=1024, tn=2048)

    biases = [_branch_bias(rel_bias, g) for g in range(N_BRANCH)]
    steps_rev = jnp.stack([b[:, :0:-1] for b in biases])[:, :, None, :]
    step0 = jnp.stack([jnp.broadcast_to(b[:, :1], (N_HEADS, LANES)) for b in biases])[:, :, None, :]
    o_p = _prompt_attention(q_p, kv_p, steps_rev, step0, batch=batch, seq=seq)

    caches = [jnp.transpose(c, (0, 2, 3, 4, 1)) for c in (cache_kv_w128, cache_kv_w512, cache_kv_w2048)]
    heads_per_step = N_HEADS
    cache_bias, new_bias = _sample_bias_tables(biases, n_new, heads_per_step)
    o_s = _sample_attention(q_s, kv_s, caches, cache_bias, new_bias, n_new=n_new, heads_per_step=heads_per_step)

    attn_ln = (vec(ln_g[1, 0]), vec(ln_b[1, 0]))
    y_s, w1b, w2b, wob = _mlp_layer_rounding(x2s, mlp_w1, mlp_w2, 1, vec(ln_g[1, 1]), vec(ln_b[1, 1]),
                                             attn=(o_s, attn_w_o[0]) + attn_ln)
    y_p = _mlp_layer(x2p, w1b, w2b, vec(ln_g[1, 1]), vec(ln_b[1, 1]), tm=512, attn=(o_p, wob) + attn_ln)

    wkv_t3 = wkv_tb.reshape(N_BRANCH, 2 * D_MODEL, D_MODEL)
    kv_p_out = []
    for g in range(N_BRANCH):
        rows = min(WINDOWS[g], seq)
        kv_t = _kv_state(x2p, wkv_t3, g, batch=batch, seq=seq, rows=rows, tw=min(rows, 512))
        kv_p_out.append(kv_t.reshape(batch, 2, N_HEADS, HEAD_DIM, rows).transpose(0, 4, 1, 2, 3))
    kv_s_t = _kv_state_sample(x2s.reshape(n_samp, n_new, D_MODEL).transpose(1, 0, 2), wkv_t3)
    kv_s_out = [kv_s_t[g].reshape(n_new, 2, N_HEADS, HEAD_DIM, n_samp).transpose(4, 0, 1, 2, 3)
                for g in range(N_BRANCH)]
    return (y_p.reshape(batch, seq, D_MODEL), y_s.reshape(n_samp, n_new, D_MODEL),
            vn_p[None], vn_s.reshape(1, n_samp, n_new, GATE),
            kv_p_out[0], kv_s_out[0], kv_p_out[1], kv_s_out[1], kv_p_out[2], kv_s_out[2])
```

```python
import functools
import math

import jax
import jax.numpy as jnp
from jax import lax
from jax.experimental import pallas as pl
from jax.experimental.pallas import tpu as pltpu

D_MODEL = 1024
CHUNK = 128
GATE = 2 * D_MODEL
SGU_GROUPS = 8
SGU_GROUP_DIM = GATE // SGU_GROUPS
WINDOWS = (128, 512, 2048)
DILATIONS = (1, 4, 16)
N_BRANCH = 3
HEAD_DIM = 64
N_HEADS = 16
D_FF = 4 * D_MODEL
NUM_BUCKETS = 32
MAX_DISTANCE = 2048
PAST_LEN = 2048
DEPTH = 2
ALPHA = (2 * DEPTH) ** 0.25
LN_EPS = 1e-5
NEG = -1e30
Q_SCALE = HEAD_DIM ** -0.5

BF16 = jnp.bfloat16
F32 = jnp.float32

VMEM_LIMIT_BYTES = 56 * 1024 * 1024
SAMPLE_ATTN_VMEM_LIMIT_BYTES = 60 * 1024 * 1024
LANES = 128
Q_TILE = 2048
KEY_BLOCK = 128
HEADS_PER_STEP = LANES // HEAD_DIM
N_PAIRS = N_HEADS // HEADS_PER_STEP
NEW_ROWS = 8


def _params(n_axes, vmem_limit_bytes=VMEM_LIMIT_BYTES):
    return pltpu.CompilerParams(dimension_semantics=("arbitrary",) * n_axes, vmem_limit_bytes=vmem_limit_bytes)


def _const_spec(shape):
    return pl.BlockSpec(shape, lambda *_: (0,) * len(shape), pipeline_mode=pl.Buffered(1))


def _ln(x, g, b):
    mu = jnp.mean(x, axis=-1, keepdims=True)
    xc = x - mu
    var = jnp.mean(xc * xc, axis=-1, keepdims=True)
    return xc * lax.rsqrt(var + LN_EPS) * g + b


def _dot(a, b):
    return jnp.dot(a, b, preferred_element_type=F32)


def _dot_nt(a, b):
    return lax.dot_general(a, b, (((1,), (1,)), ((), ())), preferred_element_type=F32)


def _gmlp_kernel(x_ref, w_in_ref, b_in_ref, gv_ref, bv_ref, ws_ref, bs_ref, w_out_ref, b_out_ref,
                 g_ref, b_ref, x1_ref, vn_ref, vnb_sc, gact_sc, *, tm, blocks_per_seq):
    x = x_ref[...]
    xb = x.astype(BF16)
    zv = _dot(xb, w_in_ref[:, GATE:]) + b_in_ref[:, GATE:]
    vn = _ln(jax.nn.gelu(zv), gv_ref[...], bv_ref[...])
    vnb_sc[...] = vn.astype(BF16)

    if blocks_per_seq is None:
        vn_ref[...] = vn
    else:
        @pl.when(pl.program_id(0) % blocks_per_seq == blocks_per_seq - 1)
        def _():
            vn_ref[0] = vn[tm - CHUNK:, :]

    row = lax.broadcasted_iota(jnp.int32, (CHUNK, CHUNK), 0)
    col = lax.broadcasted_iota(jnp.int32, (CHUNK, CHUNK), 1)
    causal = row >= col
    for g in range(SGU_GROUPS):
        lo, hi = g * SGU_GROUP_DIM, (g + 1) * SGU_GROUP_DIM
        zu = jax.nn.gelu(_dot(xb, w_in_ref[:, lo:hi]) + b_in_ref[:, lo:hi])
        wg = jnp.where(causal, ws_ref[g], 0.0).astype(BF16)
        bsg = bs_ref[:, g:g + 1]
        for c in range(tm // CHUNK):
            r0, r1 = c * CHUNK, (c + 1) * CHUNK
            s = _dot(wg, vnb_sc[r0:r1, lo:hi]) + bsg
            gact_sc[r0:r1, lo:hi] = (zu[r0:r1] * s).astype(BF16)
    y = _dot(gact_sc[...], w_out_ref[...]) + b_out_ref[...]
    x1_ref[...] = _ln(ALPHA * x + y, g_ref[...], b_ref[...])


def _gmlp_layer(x, w_in, b_in, gv, bv, ws, bs_t, w_out, b_out, g, b, *, tm, rows_per_seq):
    m = x.shape[0]
    if rows_per_seq is None:
        blocks_per_seq = None
        vn_shape = jax.ShapeDtypeStruct((m, GATE), F32)
        vn_spec = pl.BlockSpec((tm, GATE), lambda i: (i, 0))
    else:
        blocks_per_seq = rows_per_seq // tm
        vn_shape = jax.ShapeDtypeStruct((m // rows_per_seq, CHUNK, GATE), F32)
        vn_spec = pl.BlockSpec((1, CHUNK, GATE), lambda i: (i // blocks_per_seq, 0, 0))
    return pl.pallas_call(
        functools.partial(_gmlp_kernel, tm=tm, blocks_per_seq=blocks_per_seq),
        grid=(m // tm,),
        in_specs=[
            pl.BlockSpec((tm, D_MODEL), lambda i: (i, 0)),
            _const_spec((D_MODEL, 2 * GATE)), _const_spec((1, 2 * GATE)),
            _const_spec((1, GATE)), _const_spec((1, GATE)),
            _const_spec((SGU_GROUPS, CHUNK, CHUNK)), _const_spec((CHUNK, SGU_GROUPS)),
            _const_spec((GATE, D_MODEL)), _const_spec((1, D_MODEL)),
            _const_spec((1, D_MODEL)), _const_spec((1, D_MODEL)),
        ],
        out_specs=[pl.BlockSpec((tm, D_MODEL), lambda i: (i, 0)), vn_spec],
        out_shape=[jax.ShapeDtypeStruct((m, D_MODEL), F32), vn_shape],
        scratch_shapes=[pltpu.VMEM((tm, GATE), BF16), pltpu.VMEM((tm, GATE), BF16)],
        compiler_params=_params(1),
        name="gmlp_layer",
    )(x, w_in, b_in, gv, bv, ws, bs_t, w_out, b_out, g, b)


def _mlp_body(x, w1_ref, w2_ref, g_ref, b_ref, out_ref):
    xb = x.astype(BF16)
    acc = jnp.zeros_like(x)
    for j in range(D_FF // D_MODEL):
        lo, hi = j * D_MODEL, (j + 1) * D_MODEL
        h = jnp.square(jnp.maximum(_dot(xb, w1_ref[:, lo:hi]), 0.0)).astype(BF16)
        acc = acc + _dot(h, w2_ref[lo:hi, :])
    out_ref[...] = _ln(ALPHA * x + acc, g_ref[...], b_ref[...])


def _mlp_kernel(x_ref, w1_ref, w2_ref, g_ref, b_ref, out_ref):
    _mlp_body(x_ref[...], w1_ref, w2_ref, g_ref, b_ref, out_ref)


def _attn_mlp_kernel(x_ref, o_ref, wo_ref, ga_ref, ba_ref, w1_ref, w2_ref, g_ref, b_ref, out_ref):
    x = _ln(ALPHA * x_ref[...] + _dot(o_ref[...].astype(BF16), wo_ref[...]), ga_ref[...], ba_ref[...])
    _mlp_body(x, w1_ref, w2_ref, g_ref, b_ref, out_ref)


def _mlp_rounding_kernel(*refs, has_attn):
    if has_attn:
        (x_ref, o_ref, wo_ref, ga_ref, ba_ref, w1_ref, w2_ref, g_ref, b_ref,
         out_ref, w1b_ref, w2b_ref, wob_ref, x_sc, xb_sc, acc_sc) = refs
    else:
        x_ref, w1_ref, w2_ref, g_ref, b_ref, out_ref, w1b_ref, w2b_ref, x_sc, xb_sc, acc_sc = refs
    j = pl.program_id(0)

    @pl.when(j == 0)
    def _():
        x = x_ref[...]
        if has_attn:
            wob = wo_ref[...].astype(BF16)
            wob_ref[...] = wob
            x = _ln(ALPHA * x + _dot(o_ref[...].astype(BF16), wob), ga_ref[...], ba_ref[...])
        x_sc[...] = x
        xb_sc[...] = x.astype(BF16)
        acc_sc[...] = jnp.zeros_like(acc_sc)

    w1b = w1_ref[...].astype(BF16)
    w2b = w2_ref[...].astype(BF16)
    w1b_ref[...] = w1b
    w2b_ref[...] = w2b
    h = jnp.square(jnp.maximum(_dot(xb_sc[...], w1b), 0.0)).astype(BF16)
    acc_sc[...] = acc_sc[...] + _dot(h, w2b)

    @pl.when(j == pl.num_programs(0) - 1)
    def _():
        out_ref[...] = _ln(ALPHA * x_sc[...] + acc_sc[...], g_ref[...], b_ref[...])


def _mlp_layer_rounding(x, w1, w2, layer, g, b, *, attn=None):
    m = x.shape[0]
    chunk = D_MODEL
    rows = _const_spec((m, D_MODEL))
    vec = _const_spec((1, D_MODEL))
    w_specs = [pl.BlockSpec((None, D_MODEL, chunk), lambda j: (layer, 0, j)),
               pl.BlockSpec((None, chunk, D_MODEL), lambda j: (layer, j, 0)), vec, vec]
    whole = lambda *shape: pl.BlockSpec(shape, lambda j: (0,) * len(shape))
    out_specs = [whole(m, D_MODEL), pl.BlockSpec((D_MODEL, chunk), lambda j: (0, j)),
                 pl.BlockSpec((chunk, D_MODEL), lambda j: (j, 0))]
    out_shape = [jax.ShapeDtypeStruct((m, D_MODEL), F32), jax.ShapeDtypeStruct((D_MODEL, D_FF), BF16),
                 jax.ShapeDtypeStruct((D_FF, D_MODEL), BF16)]
    if attn is None:
        args, specs = (x, w1, w2, g, b), [rows] + w_specs
    else:
        o, wo, ga, ba = attn
        args = (x, o, wo, ga, ba, w1, w2, g, b)
        specs = [rows, rows, _const_spec((D_MODEL, D_MODEL)), vec, vec] + w_specs
        out_specs.append(whole(D_MODEL, D_MODEL))
        out_shape.append(jax.ShapeDtypeStruct((D_MODEL, D_MODEL), BF16))
    return pl.pallas_call(
        functools.partial(_mlp_rounding_kernel, has_attn=attn is not None),
        grid=(D_FF // chunk,), in_specs=specs, out_specs=out_specs, out_shape=out_shape,
        scratch_shapes=[pltpu.VMEM((m, D_MODEL), F32), pltpu.VMEM((m, D_MODEL), BF16), pltpu.VMEM((m, D_MODEL), F32)],
        compiler_params=_params(1), name="mlp_layer_rounding",
    )(*args)


def _mlp_layer(x, w1, w2, g, b, *, tm, attn=None):
    m = x.shape[0]
    row_spec = pl.BlockSpec((tm, D_MODEL), lambda i: (i, 0))
    vec = _const_spec((1, D_MODEL))
    mlp_specs = [_const_spec((D_MODEL, D_FF)), _const_spec((D_FF, D_MODEL)), vec, vec]
    if attn is None:
        kern, args, specs = _mlp_kernel, (x, w1, w2, g, b), [row_spec] + mlp_specs
    else:
        o, wo, ga, ba = attn
        kern = _attn_mlp_kernel
        args = (x, o, wo, ga, ba, w1, w2, g, b)
        specs = [row_spec, row_spec, _const_spec((D_MODEL, D_MODEL)), vec, vec] + mlp_specs
    return pl.pallas_call(
        kern, grid=(m // tm,), in_specs=specs, out_specs=row_spec,
        out_shape=jax.ShapeDtypeStruct((m, D_MODEL), F32),
        compiler_params=_params(1), name="mlp_layer",
    )(*args)


def _project_kernel(x_ref, wt_ref, o_ref, xb_sc):
    @pl.when(pl.program_id(1) == 0)
    def _():
        xb_sc[...] = x_ref[...].astype(BF16)

    o_ref[...] = _dot_nt(xb_sc[...], wt_ref[...])


def _project(x, w_t, *, tm, tn):
    m, k = x.shape
    n = w_t.shape[0]
    return pl.pallas_call(
        _project_kernel, grid=(m // tm, n // tn),
        in_specs=[pl.BlockSpec((tm, k), lambda i, j: (i, 0)), pl.BlockSpec((tn, k), lambda i, j: (j, 0))],
        out_specs=pl.BlockSpec((tm, tn), lambda i, j: (i, j)),
        out_shape=jax.ShapeDtypeStruct((m, n), F32),
        scratch_shapes=[pltpu.VMEM((tm, k), BF16)],
        compiler_params=_params(2), name="qkv_projection",
    )(x, w_t)


def _project_rounding_kernel(x_ref, wt_ref, o_ref, wtb_ref):
    wtb = wt_ref[...].astype(BF16)
    wtb_ref[...] = wtb
    o_ref[...] = _dot_nt(x_ref[...].astype(BF16), wtb)


def _project_rounding(x, w_t, *, tn):
    m, k = x.shape
    n = w_t.shape[0]
    return pl.pallas_call(
        _project_rounding_kernel, grid=(n // tn,),
        in_specs=[_const_spec((m, k)), pl.BlockSpec((tn, k), lambda j: (j, 0))],
        out_specs=[pl.BlockSpec((m, tn), lambda j: (0, j)), pl.BlockSpec((tn, k), lambda j: (j, 0))],
        out_shape=[jax.ShapeDtypeStruct((m, n), F32), jax.ShapeDtypeStruct((n, k), BF16)],
        compiler_params=_params(1), name="qkv_projection_rounding",
    )(x, w_t)


def _kv_state_kernel(x_ref, wt_ref, o_ref):
    o_ref[...] = _dot_nt(wt_ref[...], x_ref[...].astype(BF16))


def _kv_state(x, wkv_t, g, *, batch, seq, rows, tw):
    per_seq = seq // tw
    first = (seq - rows) // tw
    return pl.pallas_call(
        _kv_state_kernel, grid=(batch, rows // tw),
        in_specs=[pl.BlockSpec((tw, D_MODEL), lambda b, j: (b * per_seq + first + j, 0)),
                  pl.BlockSpec((None, 2 * D_MODEL, D_MODEL), lambda b, j: (g, 0, 0), pipeline_mode=pl.Buffered(1))],
        out_specs=pl.BlockSpec((None, 2 * D_MODEL, tw), lambda b, j: (b, 0, j)),
        out_shape=jax.ShapeDtypeStruct((batch, 2 * D_MODEL, rows), F32),
        compiler_params=_params(2), name="kv_state",
    )(x, wkv_t)


def _kv_state_sample(x3, wkv_t):
    n_new, nb, _ = x3.shape
    return pl.pallas_call(
        _kv_state_kernel, grid=(N_BRANCH, n_new),
        in_specs=[pl.BlockSpec((None, nb, D_MODEL), lambda g, i: (i, 0, 0)),
                  pl.BlockSpec((None, 2 * D_MODEL, D_MODEL), lambda g, i: (g, 0, 0))],
        out_specs=pl.BlockSpec((None, None, 2 * D_MODEL, nb), lambda g, i: (g, i, 0, 0)),
        out_shape=jax.ShapeDtypeStruct((N_BRANCH, n_new, 2 * D_MODEL, nb), F32),
        compiler_params=_params(2), name="kv_state_sample",
    )(x3, wkv_t)


def _attend_block(q2, ks, vs, bias, no_prev, head_a, in_prev):
    qb = (q2 * Q_SCALE).astype(BF16)
    zero = jnp.zeros_like(qb)
    qs = jnp.concatenate([jnp.where(head_a, qb, zero), jnp.where(head_a, zero, qb)], axis=0)
    s = _dot_nt(qs, ks.astype(BF16)) + bias
    if no_prev is not None:
        s = s + jnp.where(in_prev, no_prev, 0.0)
    m = jnp.max(s, axis=-1, keepdims=True)
    p = jnp.exp(s - m)
    vs_aug = jnp.concatenate([vs.astype(BF16), jnp.ones((2 * KEY_BLOCK, LANES), BF16)], axis=1)
    ov = _dot(p.astype(BF16), vs_aug)
    num2 = jnp.where(head_a, ov[:KEY_BLOCK, :LANES], ov[KEY_BLOCK:, :LANES])
    den2 = jnp.where(head_a, ov[:KEY_BLOCK, LANES:], ov[KEY_BLOCK:, LANES:])
    l2 = jnp.where(head_a, m[:KEY_BLOCK], m[KEY_BLOCK:]) + jnp.log(den2)
    return num2 / den2, l2


def _prompt_attn_kernel(*refs, max_group):
    ins, (steps_ref, step0_ref, o_ref), scr = refs[:15], refs[15:18], refs[18:]
    oscs, lscs, bias_sc = scr[0:3], scr[3:6], scr[6]
    t = pl.program_id(2)
    lane = lax.broadcasted_iota(jnp.int32, (KEY_BLOCK, LANES), 1)
    qrow = lax.broadcasted_iota(jnp.int32, (KEY_BLOCK, LANES), 0)
    head_a = lane < HEAD_DIM
    kcol = lax.broadcasted_iota(jnp.int32, (2 * KEY_BLOCK, 2 * KEY_BLOCK), 1)
    in_prev = kcol < KEY_BLOCK
    no_prev_first = jnp.where(t == 0, NEG, 0.0)

    @pl.when(jnp.logical_and(pl.program_id(1) == 0, t == 0))
    def _():
        for g in range(N_BRANCH):
            for hd in range(HEADS_PER_STEP):
                steps = jnp.broadcast_to(steps_ref[g, hd], (KEY_BLOCK, LANES))
                rolled = pltpu.roll(steps, 0, 1, stride=1, stride_axis=0)
                step0 = jnp.broadcast_to(step0_ref[g, hd], (KEY_BLOCK, LANES))
                left = jnp.where(lane >= qrow, rolled, NEG)
                right = jnp.where(lane < qrow, rolled, jnp.where(lane == qrow, step0, NEG))
                bias_sc[g, hd * KEY_BLOCK:(hd + 1) * KEY_BLOCK, :] = jnp.concatenate([left, right], axis=1)

    pooled, loops = [], []
    for g, d in enumerate(DILATIONS):
        q_ref, kc_ref, vc_ref, kp_ref, vp_ref = ins[5 * g:5 * g + 5]
        nqb = Q_TILE // d // KEY_BLOCK

        def rows(start, size, d=d):
            if d == 1:
                return pl.ds(start if isinstance(start, int) else pl.multiple_of(start, KEY_BLOCK), size)
            return pl.ds(start, size, stride=d)

        def load_first(r, q_ref=q_ref, kc_ref=kc_ref, vc_ref=vc_ref, kp_ref=kp_ref, vp_ref=vp_ref, rows=rows):
            sel = rows(r, KEY_BLOCK)
            ks = jnp.concatenate([kp_ref[sel, :], kc_ref[sel, :]], axis=0)
            vs = jnp.concatenate([vp_ref[sel, :], vc_ref[sel, :]], axis=0)
            return sel, q_ref[sel, :], ks, vs

        def load_later(sp, d=d, nqb=nqb, q_ref=q_ref, kc_ref=kc_ref, vc_ref=vc_ref, rows=rows):
            r = sp // (nqb - 1)
            qb = 1 + sp % (nqb - 1)
            q_sel = rows(d * KEY_BLOCK * qb + r, KEY_BLOCK)
            k_sel = rows(d * KEY_BLOCK * (qb - 1) + r, 2 * KEY_BLOCK)
            return q_sel, q_ref[q_sel, :], kc_ref[k_sel, :], vc_ref[k_sel, :]

        if d < max_group:
            pooled += [(g, load_first, r, no_prev_first) for r in range(d)]
        else:
            loops.append((g, d, load_first, no_prev_first))
        if nqb > 1:
            loops.append((g, d * (nqb - 1), load_later, None))

    def attend_group(items):
        loaded = [(g, no_prev) + load(idx) for g, load, idx, no_prev in items]
        done = [(g, sel, _attend_block(q2, ks, vs, bias_sc[g], no_prev, head_a, in_prev))
                for g, no_prev, sel, q2, ks, vs in loaded]
        for g, sel, (o2, l2) in done:
            oscs[g][sel, :] = o2
            lscs[g][sel, :] = l2

    for i in range(0, len(pooled), max_group):
        attend_group(pooled[i:i + max_group])
    for g, n_blocks, load, no_prev in loops:
        group = max(u for u in range(1, max_group + 1) if n_blocks % u == 0)

        def body(it, carry, g=g, load=load, no_prev=no_prev, group=group):
            attend_group([(g, load, it * group + u, no_prev) for u in range(group)])
            return carry

        lax.fori_loop(0, n_blocks // group, body, 0)

    merge_rows = 256
    for c in range(Q_TILE // merge_rows):
        rs = slice(c * merge_rows, (c + 1) * merge_rows)
        ls = [lscs[g][rs, :] for g in range(N_BRANCH)]
        m = jnp.maximum(jnp.maximum(ls[0], ls[1]), ls[2])
        es = [jnp.exp(l - m) for l in ls]
        num = es[0] * oscs[0][rs, :] + es[1] * oscs[1][rs, :] + es[2] * oscs[2][rs, :]
        o_ref[rs, :] = (num / (es[0] + es[1] + es[2])).astype(o_ref.dtype)


def _prompt_attention(q, kv, steps_rev, step0, *, batch, seq, max_group=6):
    n_tiles = seq // Q_TILE
    in_specs = []
    for g, d in enumerate(DILATIONS):
        prev_rows = KEY_BLOCK * d
        per_tile = Q_TILE // prev_rows
        q_col = g * N_PAIRS
        k_col = 2 * g * N_PAIRS
        v_col = k_col + N_PAIRS

        def cur(col):
            return pl.BlockSpec((Q_TILE, LANES), lambda hp, b, t, col=col: (b * n_tiles + t, col + hp))

        def prev(col, per_tile=per_tile, prev_rows=prev_rows):
            return pl.BlockSpec(
                (prev_rows, LANES),
                lambda hp, b, t, col=col: (jnp.maximum((b * n_tiles + t) * per_tile - 1, 0), col + hp))

        in_specs += [cur(q_col), cur(k_col), cur(v_col), prev(k_col), prev(v_col)]
    head_vec = pl.BlockSpec((N_BRANCH, HEADS_PER_STEP, 1, LANES), lambda hp, b, t: (0, hp, 0, 0))
    in_specs += [head_vec, head_vec]
    return pl.pallas_call(
        functools.partial(_prompt_attn_kernel, max_group=max_group),
        grid=(N_PAIRS, batch, n_tiles),
        in_specs=in_specs,
        out_specs=pl.BlockSpec((Q_TILE, LANES), lambda hp, b, t: (b * n_tiles + t, hp)),
        out_shape=jax.ShapeDtypeStruct((batch * seq, D_MODEL), BF16),
        scratch_shapes=[pltpu.VMEM((Q_TILE, LANES), F32)] * (2 * N_BRANCH)
        + [pltpu.VMEM((N_BRANCH, 2 * KEY_BLOCK, 2 * KEY_BLOCK), F32)],
        compiler_params=_params(3), name="prompt_attention",
    )(*([q, kv, kv, kv, kv] * N_BRANCH), steps_rev, step0)


def _sample_attn_kernel(*refs, n_new, heads):
    qkv_refs, caches, cbias, bn_ref, o_ref = refs[:9], refs[9:12], refs[12:15], refs[15], refs[16]
    j = pl.program_id(1)
    which = pl.program_id(2)
    width = heads * HEAD_DIM
    per_block = NEW_ROWS // n_new

    def own_rows(ref):
        rows = ref[0:n_new, :]
        for w in range(1, per_block):
            rows = jnp.where(which == w, ref[w * n_new:(w + 1) * n_new, :], rows)
        return rows

    pad = jnp.zeros((NEW_ROWS - n_new, width), F32)
    lane_head = lax.broadcasted_iota(jnp.int32, (heads, width), 1) // HEAD_DIM
    own_lanes = lane_head == lax.broadcasted_iota(jnp.int32, (heads, width), 0)

    def per_head_rows(x):
        return jnp.concatenate(
            [jnp.sum(jnp.where(own_lanes, x[i * heads:(i + 1) * heads], 0.0), axis=0, keepdims=True)
             for i in range(n_new)], axis=0)

    outs, lses = [], []
    for g in range(N_BRANCH):
        q = own_rows(qkv_refs[3 * g]) * Q_SCALE
        k_new = jnp.concatenate([own_rows(qkv_refs[3 * g + 1]), pad], axis=0).astype(BF16)
        v_new = jnp.concatenate([own_rows(qkv_refs[3 * g + 2]), pad], axis=0).astype(BF16)
        qbd = jnp.concatenate([jnp.where(own_lanes, q[i:i + 1, :], 0.0) for i in range(n_new)],
                              axis=0).astype(BF16)
        n_pos = caches[g].shape[-1]
        k_t = caches[g][0, 0].reshape(width, n_pos).astype(BF16)
        v_t = caches[g][0, 1].reshape(width, n_pos).astype(BF16)
        s_c = _dot(qbd, k_t) + cbias[g][j]
        s_n = _dot_nt(qbd, k_new) + bn_ref[g, j]
        m = jnp.maximum(jnp.max(s_c, axis=-1, keepdims=True), jnp.max(s_n, axis=-1, keepdims=True))
        p_c = jnp.exp(s_c - m)
        p_n = jnp.exp(s_n - m)
        den = jnp.sum(p_c, axis=-1, keepdims=True) + jnp.sum(p_n, axis=-1, keepdims=True)
        o = (_dot_nt(p_c.astype(BF16), v_t) + _dot(p_n.astype(BF16), v_new)) / den
        outs.append(per_head_rows(o))
        lses.append(per_head_rows(jnp.broadcast_to(m + jnp.log(den), o.shape)))
    m = jnp.maximum(jnp.maximum(lses[0], lses[1]), lses[2])
    es = [jnp.exp(l - m) for l in lses]
    num = es[0] * outs[0] + es[1] * outs[1] + es[2] * outs[2]
    merged = num / (es[0] + es[1] + es[2])
    for w in range(per_block):
        @pl.when(which == w)
        def _(w=w):
            o_ref[w * n_new:(w + 1) * n_new, :] = merged


def _sample_attention(q, kv, caches, cache_bias, new_bias, *, n_new, heads_per_step=8):
    hs = heads_per_step
    width = hs * HEAD_DIM
    per_block = NEW_ROWS // n_new
    n_blocks = q.shape[0] // NEW_ROWS
    window = lambda col: pl.BlockSpec((NEW_ROWS, width), lambda r, j, w, col=col: (r, col + j))
    per_d = D_MODEL // width
    windows = []
    for g in range(N_BRANCH):
        windows += [window(g * per_d), window(2 * g * per_d), window((2 * g + 1) * per_d)]
    return pl.pallas_call(
        functools.partial(_sample_attn_kernel, n_new=n_new, heads=hs),
        grid=(n_blocks, N_HEADS // hs, per_block),
        in_specs=windows
        + [pl.BlockSpec((1, 2, hs, HEAD_DIM, c.shape[-1]), lambda r, j, w: (r * per_block + w, 0, j, 0, 0))
           for c in caches]
        + [_const_spec(t.shape) for t in cache_bias] + [_const_spec(new_bias.shape)],
        out_specs=pl.BlockSpec((NEW_ROWS, width), lambda r, j, w: (r, j)),
        out_shape=jax.ShapeDtypeStruct((q.shape[0], D_MODEL), F32),
        compiler_params=_params(3, SAMPLE_ATTN_VMEM_LIMIT_BYTES), name="sample_attention",
    )(*([q, kv, kv] * N_BRANCH), *caches, *cache_bias, new_bias)


def _t5_bucket(dist):
    max_exact = NUM_BUCKETS // 2
    n = jnp.maximum(dist, 1).astype(F32)
    large = max_exact + (jnp.log(n / max_exact) / math.log(MAX_DISTANCE / max_exact)
                         * (NUM_BUCKETS - max_exact)).astype(jnp.int32)
    large = jnp.minimum(large, NUM_BUCKETS - 1)
    return jnp.where(dist < max_exact, dist, large)


def _branch_bias(rel_bias, g):
    n_keys = WINDOWS[g] // DILATIONS[g] + 1
    dist = jnp.arange(n_keys, dtype=jnp.int32) * DILATIONS[g]
    return rel_bias[_t5_bucket(dist)][:, g * N_HEADS:(g + 1) * N_HEADS].T.astype(F32)


def _sample_bias_tables(biases, n_new, hs):
    neg = lambda *shape: jnp.full(shape, NEG, F32)
    by_step = lambda t: t.reshape(N_HEADS // hs, hs, n_new, -1).transpose(0, 2, 1, 3).reshape(
        N_HEADS // hs, n_new * hs, -1)
    cache_tabs, new_tabs = [], []
    for g, d in enumerate(DILATIONS):
        w = WINDOWS[g]
        n_steps = w // d
        b = biases[g]
        rows, new_rows = [], []
        for i in range(n_new):
            first = n_steps + i // d
            hit = b[:, min(first, n_steps):i // d:-1]
            hit = jnp.concatenate([neg(N_HEADS, n_steps - hit.shape[1]), hit], axis=1)
            grid = jnp.where((jnp.arange(d) == i % d)[None, None, :], hit[:, :, None], NEG)
            rows.append(grid.reshape(N_HEADS, w))
            cols = [b[:, (i - j) // d] if (j <= i and (i - j) % d == 0 and j < n_new) else neg(N_HEADS)
                    for j in range(NEW_ROWS)]
            new_rows.append(jnp.stack(cols, axis=1))
        cache_tabs.append(by_step(jnp.stack(rows, axis=1)))
        new_tabs.append(by_step(jnp.stack(new_rows, axis=1)))
    return cache_tabs, jnp.stack(new_tabs)


def kernel(x_prompt, x_sample, cache_kv_w128, cache_kv_w512, cache_kv_w2048, ln_g, ln_b, gm_w_in, gm_b_in,
           gm_ln_g, gm_ln_b, gm_w_s, gm_b_s, gm_w_out, gm_b_out, w_kv, attn_w_q, attn_w_o, rel_bias,
           mlp_w1, mlp_w2):
    batch, seq, _ = x_prompt.shape
    n_samp, n_new, _ = x_sample.shape
    ms = n_samp * n_new
    assert DEPTH == 2 and gm_w_in.shape[0] == 1 and attn_w_q.shape[0] == 1
    assert seq % Q_TILE == 0 and ms % CHUNK == 0 and CHUNK % n_new == 0 and NEW_ROWS % n_new == 0
    assert all(PAST_LEN >= w and w % d == 0 and w // d == KEY_BLOCK for w, d in zip(WINDOWS, DILATIONS))

    bf = lambda w: w.astype(BF16)
    vec = lambda v: v.reshape(1, -1)
    w_in, w_out = bf(gm_w_in[0]), bf(gm_w_out[0])
    wq_t = jnp.transpose(attn_w_q[0], (1, 2, 3, 0)).reshape(N_BRANCH * D_MODEL, D_MODEL)
    wkv_t = jnp.transpose(w_kv, (1, 2, 3, 4, 0)).reshape(N_BRANCH * 2 * D_MODEL, D_MODEL)
    gm_args = (vec(gm_b_in[0]), vec(gm_ln_g[0]), vec(gm_ln_b[0]))
    gm_tail = (w_out, vec(gm_b_out[0]), vec(ln_g[0, 0]), vec(ln_b[0, 0]))

    xp = x_prompt.reshape(batch * seq, D_MODEL)
    xs = x_sample.reshape(ms, D_MODEL)
    x1p, vn_p = _gmlp_layer(xp, w_in, *gm_args, gm_w_s[0], gm_b_s[0].T, *gm_tail, tm=512, rows_per_seq=seq)
    reps = CHUNK // n_new
    same_seq = (jnp.arange(CHUNK)[:, None] // n_new) == (jnp.arange(CHUNK)[None, :] // n_new)
    ws_s = jnp.where(same_seq[None], jnp.tile(gm_w_s[0][:, :n_new, :n_new], (1, reps, reps)), 0.0)
    bs_s = jnp.tile(gm_b_s[0][:, :n_new], (1, reps)).T
    x1s, vn_s = _gmlp_layer(xs, w_in, *gm_args, ws_s, bs_s, *gm_tail, tm=ms, rows_per_seq=None)

    x2s, w1b, w2b = _mlp_layer_rounding(x1s, mlp_w1, mlp_w2, 0, vec(ln_g[0, 1]), vec(ln_b[0, 1]))
    x2p = _mlp_layer(x1p, w1b, w2b, vec(ln_g[0, 1]), vec(ln_b[0, 1]), tm=512)

    q_s, wq_tb = _project_rounding(x2s, wq_t, tn=D_MODEL)
    kv_s, wkv_tb = _project_rounding(x2s, wkv_t, tn=D_MODEL)
    q_p = _project(x2p, wq_tb, tm=1024, tn=3072)
    kv_p = _project(x2p, wkv_tb, tm=1024, tn=3072)

    biases = [_branch_bias(rel_bias, g) for g in range(N_BRANCH)]
    steps_rev = jnp.stack([b[:, :0:-1] for b in biases])[:, :, None, :]
    step0 = jnp.stack([jnp.broadcast_to(b[:, :1], (N_HEADS, LANES)) for b in biases])[:, :, None, :]
    o_p = _prompt_attention(q_p, kv_p, steps_rev, step0, batch=batch, seq=seq)

    caches = [jnp.transpose(c, (0, 2, 3, 4, 1)) for c in (cache_kv_w128, cache_kv_w512, cache_kv_w2048)]
    heads_per_step = N_HEADS
    cache_bias, new_bias = _sample_bias_tables(biases, n_new, heads_per_step)
    o_s = _sample_attention(q_s, kv_s, caches, cache_bias, new_bias, n_new=n_new, heads_per_step=heads_per_step)

    attn_ln = (vec(ln_g[1, 0]), vec(ln_b[1, 0]))
    y_s, w1b, w2b, wob = _mlp_layer_rounding(x2s, mlp_w1, mlp_w2, 1, vec(ln_g[1, 1]), vec(ln_b[1, 1]),
                                             attn=(o_s, attn_w_o[0]) + attn_ln)
    y_p = _mlp_layer(x2p, w1b, w2b, vec(ln_g[1, 1]), vec(ln_b[1, 1]), tm=512, attn=(o_p, wob) + attn_ln)

    wkv_t3 = wkv_tb.reshape(N_BRANCH, 2 * D_MODEL, D_MODEL)
    kv_p_out = []
    for g in range(N_BRANCH):
        rows = min(WINDOWS[g], seq)
        kv_t = _kv_state(x2p, wkv_t3, g, batch=batch, seq=seq, rows=rows, tw=min(rows, 512))
        kv_p_out.append(kv_t.reshape(batch, 2, N_HEADS, HEAD_DIM, rows).transpose(0, 4, 1, 2, 3))
    kv_s_t = _kv_state_sample(x2s.reshape(n_samp, n_new, D_MODEL).transpose(1, 0, 2), wkv_t3)
    kv_s_out = [kv_s_t[g].reshape(n_new, 2, N_HEADS, HEAD_DIM, n_samp).transpose(4, 0, 1, 2, 3)
                for g in range(N_BRANCH)]
    return (y_p.reshape(batch, seq, D_MODEL), y_s.reshape(n_samp, n_new, D_MODEL),
            vn_p[None], vn_s.reshape(1, n_samp, n_new, GATE),
            kv_p_out[0], kv_s_out[0], kv_p_out[1], kv_s_out[1], kv_p_out[2], kv_s_out[2])
```

```python
import functools
import math

import jax
import jax.numpy as jnp
from jax import lax
from jax.experimental import pallas as pl
from jax.experimental.pallas import tpu as pltpu

D_MODEL = 1024
CHUNK = 128
GATE = 2 * D_MODEL
SGU_GROUPS = 8
SGU_GROUP_DIM = GATE // SGU_GROUPS
WINDOWS = (128, 512, 2048)
DILATIONS = (1, 4, 16)
N_BRANCH = 3
HEAD_DIM = 64
N_HEADS = 16
D_FF = 4 * D_MODEL
NUM_BUCKETS = 32
MAX_DISTANCE = 2048
PAST_LEN = 2048
DEPTH = 2
ALPHA = (2 * DEPTH) ** 0.25
LN_EPS = 1e-5
NEG = -1e30
Q_SCALE = HEAD_DIM ** -0.5

BF16 = jnp.bfloat16
F32 = jnp.float32

MIB = 1024 * 1024
V7X_VMEM_BYTES = 64 * MIB
VMEM_LIMIT_BYTES = V7X_VMEM_BYTES - 8 * MIB
SAMPLE_ATTN_VMEM_LIMIT_BYTES = V7X_VMEM_BYTES - 4 * MIB
LANES = 128
Q_TILE = 2048
KEY_BLOCK = 128
HEADS_PER_STEP = LANES // HEAD_DIM
N_PAIRS = N_HEADS // HEADS_PER_STEP
NEW_ROWS = 8

GMLP_ROWS = 512
MLP_ROWS = 512
PROJ_ROWS = 1024
PROJ_COLS = 3 * D_MODEL
ROUNDING_COLS = D_MODEL
KV_STATE_COLS = 1024
ATTN_GROUP = 6


def _params(n_axes, vmem_limit_bytes=VMEM_LIMIT_BYTES):
    return pltpu.CompilerParams(dimension_semantics=("arbitrary",) * n_axes, vmem_limit_bytes=vmem_limit_bytes)


def _const_spec(shape):
    return pl.BlockSpec(shape, lambda *_: (0,) * len(shape), pipeline_mode=pl.Buffered(1))


def _ln(x, g, b):
    mu = jnp.mean(x, axis=-1, keepdims=True)
    xc = x - mu
    var = jnp.mean(xc * xc, axis=-1, keepdims=True)
    return xc * lax.rsqrt(var + LN_EPS) * g + b


def _dot(a, b):
    return jnp.dot(a, b, preferred_element_type=F32)


def _dot_nt(a, b):
    return lax.dot_general(a, b, (((1,), (1,)), ((), ())), preferred_element_type=F32)


def _gmlp_kernel(x_ref, w_in_ref, b_in_ref, gv_ref, bv_ref, ws_ref, bs_ref, w_out_ref, b_out_ref,
                 g_ref, b_ref, x1_ref, vn_ref, vnb_sc, gact_sc, *, tm, blocks_per_seq):
    x = x_ref[...]
    xb = x.astype(BF16)
    zv = _dot(xb, w_in_ref[:, GATE:]) + b_in_ref[:, GATE:]
    vn = _ln(jax.nn.gelu(zv), gv_ref[...], bv_ref[...])
    vnb_sc[...] = vn.astype(BF16)

    if blocks_per_seq is None:
        vn_ref[...] = vn
    else:
        @pl.when(pl.program_id(0) % blocks_per_seq == blocks_per_seq - 1)
        def _():
            vn_ref[0] = vn[tm - CHUNK:, :]

    row = lax.broadcasted_iota(jnp.int32, (CHUNK, CHUNK), 0)
    col = lax.broadcasted_iota(jnp.int32, (CHUNK, CHUNK), 1)
    causal = row >= col
    for g in range(SGU_GROUPS):
        lo, hi = g * SGU_GROUP_DIM, (g + 1) * SGU_GROUP_DIM
        zu = jax.nn.gelu(_dot(xb, w_in_ref[:, lo:hi]) + b_in_ref[:, lo:hi])
        wg = jnp.where(causal, ws_ref[g], 0.0).astype(BF16)
        bsg = bs_ref[:, g:g + 1]
        for c in range(tm // CHUNK):
            r0, r1 = c * CHUNK, (c + 1) * CHUNK
            s = _dot(wg, vnb_sc[r0:r1, lo:hi]) + bsg
            gact_sc[r0:r1, lo:hi] = (zu[r0:r1] * s).astype(BF16)
    y = _dot(gact_sc[...], w_out_ref[...]) + b_out_ref[...]
    x1_ref[...] = _ln(ALPHA * x + y, g_ref[...], b_ref[...])


def _gmlp_layer(x, w_in, b_in, gv, bv, ws, bs_t, w_out, b_out, g, b, *, tm, rows_per_seq):
    m = x.shape[0]
    if rows_per_seq is None:
        blocks_per_seq = None
        vn_shape = jax.ShapeDtypeStruct((m, GATE), F32)
        vn_spec = pl.BlockSpec((tm, GATE), lambda i: (i, 0))
    else:
        blocks_per_seq = rows_per_seq // tm
        vn_shape = jax.ShapeDtypeStruct((m // rows_per_seq, CHUNK, GATE), F32)
        vn_spec = pl.BlockSpec((1, CHUNK, GATE), lambda i: (i // blocks_per_seq, 0, 0))
    return pl.pallas_call(
        functools.partial(_gmlp_kernel, tm=tm, blocks_per_seq=blocks_per_seq),
        grid=(m // tm,),
        in_specs=[
            pl.BlockSpec((tm, D_MODEL), lambda i: (i, 0)),
            _const_spec((D_MODEL, 2 * GATE)), _const_spec((1, 2 * GATE)),
            _const_spec((1, GATE)), _const_spec((1, GATE)),
            _const_spec((SGU_GROUPS, CHUNK, CHUNK)), _const_spec((CHUNK, SGU_GROUPS)),
            _const_spec((GATE, D_MODEL)), _const_spec((1, D_MODEL)),
            _const_spec((1, D_MODEL)), _const_spec((1, D_MODEL)),
        ],
        out_specs=[pl.BlockSpec((tm, D_MODEL), lambda i: (i, 0)), vn_spec],
        out_shape=[jax.ShapeDtypeStruct((m, D_MODEL), F32), vn_shape],
        scratch_shapes=[pltpu.VMEM((tm, GATE), BF16), pltpu.VMEM((tm, GATE), BF16)],
        compiler_params=_params(1),
        name="gmlp_layer",
    )(x, w_in, b_in, gv, bv, ws, bs_t, w_out, b_out, g, b)


def _mlp_body(x, w1_ref, w2_ref, g_ref, b_ref, out_ref):
    xb = x.astype(BF16)
    acc = jnp.zeros_like(x)
    for j in range(D_FF // D_MODEL):
        lo, hi = j * D_MODEL, (j + 1) * D_MODEL
        h = jnp.square(jnp.maximum(_dot(xb, w1_ref[:, lo:hi]), 0.0)).astype(BF16)
        acc = acc + _dot(h, w2_ref[lo:hi, :])
    out_ref[...] = _ln(ALPHA * x + acc, g_ref[...], b_ref[...])


def _mlp_kernel(x_ref, w1_ref, w2_ref, g_ref, b_ref, out_ref):
    _mlp_body(x_ref[...], w1_ref, w2_ref, g_ref, b_ref, out_ref)


def _attn_mlp_kernel(x_ref, o_ref, wo_ref, ga_ref, ba_ref, w1_ref, w2_ref, g_ref, b_ref, out_ref):
    x = _ln(ALPHA * x_ref[...] + _dot(o_ref[...].astype(BF16), wo_ref[...]), ga_ref[...], ba_ref[...])
    _mlp_body(x, w1_ref, w2_ref, g_ref, b_ref, out_ref)


def _mlp_rounding_kernel(*refs, has_attn):
    if has_attn:
        (x_ref, o_ref, wo_ref, ga_ref, ba_ref, w1_ref, w2_ref, g_ref, b_ref,
         out_ref, w1b_ref, w2b_ref, wob_ref, x_sc, xb_sc, acc_sc) = refs
    else:
        x_ref, w1_ref, w2_ref, g_ref, b_ref, out_ref, w1b_ref, w2b_ref, x_sc, xb_sc, acc_sc = refs
    j = pl.program_id(0)

    @pl.when(j == 0)
    def _():
        x = x_ref[...]
        if has_attn:
            wob = wo_ref[...].astype(BF16)
            wob_ref[...] = wob
            x = _ln(ALPHA * x + _dot(o_ref[...].astype(BF16), wob), ga_ref[...], ba_ref[...])
        x_sc[...] = x
        xb_sc[...] = x.astype(BF16)
        acc_sc[...] = jnp.zeros_like(acc_sc)

    w1b = w1_ref[...].astype(BF16)
    w2b = w2_ref[...].astype(BF16)
    w1b_ref[...] = w1b
    w2b_ref[...] = w2b
    h = jnp.square(jnp.maximum(_dot(xb_sc[...], w1b), 0.0)).astype(BF16)
    acc_sc[...] = acc_sc[...] + _dot(h, w2b)

    @pl.when(j == pl.num_programs(0) - 1)
    def _():
        out_ref[...] = _ln(ALPHA * x_sc[...] + acc_sc[...], g_ref[...], b_ref[...])


def _mlp_layer_rounding(x, w1, w2, layer, g, b, *, attn=None):
    m = x.shape[0]
    chunk = ROUNDING_COLS
    rows = _const_spec((m, D_MODEL))
    vec = _const_spec((1, D_MODEL))
    w_specs = [pl.BlockSpec((None, D_MODEL, chunk), lambda j: (layer, 0, j)),
               pl.BlockSpec((None, chunk, D_MODEL), lambda j: (layer, j, 0)), vec, vec]
    whole = lambda *shape: pl.BlockSpec(shape, lambda j: (0,) * len(shape))
    out_specs = [whole(m, D_MODEL), pl.BlockSpec((D_MODEL, chunk), lambda j: (0, j)),
                 pl.BlockSpec((chunk, D_MODEL), lambda j: (j, 0))]
    out_shape = [jax.ShapeDtypeStruct((m, D_MODEL), F32), jax.ShapeDtypeStruct((D_MODEL, D_FF), BF16),
                 jax.ShapeDtypeStruct((D_FF, D_MODEL), BF16)]
    if attn is None:
        args, specs = (x, w1, w2, g, b), [rows] + w_specs
    else:
        o, wo, ga, ba = attn
        args = (x, o, wo, ga, ba, w1, w2, g, b)
        specs = [rows, rows, _const_spec((D_MODEL, D_MODEL)), vec, vec] + w_specs
        out_specs.append(whole(D_MODEL, D_MODEL))
        out_shape.append(jax.ShapeDtypeStruct((D_MODEL, D_MODEL), BF16))
    return pl.pallas_call(
        functools.partial(_mlp_rounding_kernel, has_attn=attn is not None),
        grid=(D_FF // chunk,), in_specs=specs, out_specs=out_specs, out_shape=out_shape,
        scratch_shapes=[pltpu.VMEM((m, D_MODEL), F32), pltpu.VMEM((m, D_MODEL), BF16), pltpu.VMEM((m, D_MODEL), F32)],
        compiler_params=_params(1), name="mlp_layer_rounding",
    )(*args)


def _mlp_layer(x, w1, w2, g, b, *, tm, attn=None):
    m = x.shape[0]
    row_spec = pl.BlockSpec((tm, D_MODEL), lambda i: (i, 0))
    vec = _const_spec((1, D_MODEL))
    mlp_specs = [_const_spec((D_MODEL, D_FF)), _const_spec((D_FF, D_MODEL)), vec, vec]
    if attn is None:
        kern, args, specs = _mlp_kernel, (x, w1, w2, g, b), [row_spec] + mlp_specs
    else:
        o, wo, ga, ba = attn
        kern = _attn_mlp_kernel
        args = (x, o, wo, ga, ba, w1, w2, g, b)
        specs = [row_spec, row_spec, _const_spec((D_MODEL, D_MODEL)), vec, vec] + mlp_specs
    return pl.pallas_call(
        kern, grid=(m // tm,), in_specs=specs, out_specs=row_spec,
        out_shape=jax.ShapeDtypeStruct((m, D_MODEL), F32),
        compiler_params=_params(1), name="mlp_layer",
    )(*args)


def _project_kernel(x_ref, wt_ref, o_ref, xb_sc):
    @pl.when(pl.program_id(1) == 0)
    def _():
        xb_sc[...] = x_ref[...].astype(BF16)

    o_ref[...] = _dot_nt(xb_sc[...], wt_ref[...])


def _project(x, w_t, *, tm, tn):
    m, k = x.shape
    n = w_t.shape[0]
    return pl.pallas_call(
        _project_kernel, grid=(m // tm, n // tn),
        in_specs=[pl.BlockSpec((tm, k), lambda i, j: (i, 0)), pl.BlockSpec((tn, k), lambda i, j: (j, 0))],
        out_specs=pl.BlockSpec((tm, tn), lambda i, j: (i, j)),
        out_shape=jax.ShapeDtypeStruct((m, n), F32),
        scratch_shapes=[pltpu.VMEM((tm, k), BF16)],
        compiler_params=_params(2), name="qkv_projection",
    )(x, w_t)


def _project_rounding_kernel(x_ref, wt_ref, o_ref, wtb_ref):
    wtb = wt_ref[...].astype(BF16)
    wtb_ref[...] = wtb
    o_ref[...] = _dot_nt(x_ref[...].astype(BF16), wtb)


def _project_rounding(x, w_t, *, tn):
    m, k = x.shape
    n = w_t.shape[0]
    return pl.pallas_call(
        _project_rounding_kernel, grid=(n // tn,),
        in_specs=[_const_spec((m, k)), pl.BlockSpec((tn, k), lambda j: (j, 0))],
        out_specs=[pl.BlockSpec((m, tn), lambda j: (0, j)), pl.BlockSpec((tn, k), lambda j: (j, 0))],
        out_shape=[jax.ShapeDtypeStruct((m, n), F32), jax.ShapeDtypeStruct((n, k), BF16)],
        compiler_params=_params(1), name="qkv_projection_rounding",
    )(x, w_t)


def _kv_state_kernel(x_ref, wt_ref, o_ref):
    o_ref[...] = _dot_nt(wt_ref[...], x_ref[...].astype(BF16))


def _kv_state(x, wkv_t, g, *, batch, seq, rows, tw):
    per_seq = seq // tw
    first = (seq - rows) // tw
    return pl.pallas_call(
        _kv_state_kernel, grid=(batch, rows // tw),
        in_specs=[pl.BlockSpec((tw, D_MODEL), lambda b, j: (b * per_seq + first + j, 0)),
                  pl.BlockSpec((None, 2 * D_MODEL, D_MODEL), lambda b, j: (g, 0, 0), pipeline_mode=pl.Buffered(1))],
        out_specs=pl.BlockSpec((None, 2 * D_MODEL, tw), lambda b, j: (b, 0, j)),
        out_shape=jax.ShapeDtypeStruct((batch, 2 * D_MODEL, rows), F32),
        compiler_params=_params(2), name="kv_state",
    )(x, wkv_t)


def _kv_state_sample(x3, wkv_t):
    n_new, nb, _ = x3.shape
    return pl.pallas_call(
        _kv_state_kernel, grid=(N_BRANCH, n_new),
        in_specs=[pl.BlockSpec((None, nb, D_MODEL), lambda g, i: (i, 0, 0)),
                  pl.BlockSpec((None, 2 * D_MODEL, D_MODEL), lambda g, i: (g, 0, 0))],
        out_specs=pl.BlockSpec((None, None, 2 * D_MODEL, nb), lambda g, i: (g, i, 0, 0)),
        out_shape=jax.ShapeDtypeStruct((N_BRANCH, n_new, 2 * D_MODEL, nb), F32),
        compiler_params=_params(2), name="kv_state_sample",
    )(x3, wkv_t)


def _attend_block(q2, ks, vs, bias, no_prev, head_a, in_prev):
    qb = (q2 * Q_SCALE).astype(BF16)
    zero = jnp.zeros_like(qb)
    qs = jnp.concatenate([jnp.where(head_a, qb, zero), jnp.where(head_a, zero, qb)], axis=0)
    s = _dot_nt(qs, ks.astype(BF16)) + bias
    if no_prev is not None:
        s = s + jnp.where(in_prev, no_prev, 0.0)
    m = jnp.max(s, axis=-1, keepdims=True)
    p = jnp.exp(s - m)
    vs_aug = jnp.concatenate([vs.astype(BF16), jnp.ones((2 * KEY_BLOCK, LANES), BF16)], axis=1)
    ov = _dot(p.astype(BF16), vs_aug)
    num2 = jnp.where(head_a, ov[:KEY_BLOCK, :LANES], ov[KEY_BLOCK:, :LANES])
    den2 = jnp.where(head_a, ov[:KEY_BLOCK, LANES:], ov[KEY_BLOCK:, LANES:])
    l2 = jnp.where(head_a, m[:KEY_BLOCK], m[KEY_BLOCK:]) + jnp.log(den2)
    return num2 / den2, l2


def _prompt_attn_kernel(*refs, max_group):
    ins, (steps_ref, step0_ref, o_ref), scr = refs[:15], refs[15:18], refs[18:]
    oscs, lscs, bias_sc = scr[0:3], scr[3:6], scr[6]
    t = pl.program_id(2)
    lane = lax.broadcasted_iota(jnp.int32, (KEY_BLOCK, LANES), 1)
    qrow = lax.broadcasted_iota(jnp.int32, (KEY_BLOCK, LANES), 0)
    head_a = lane < HEAD_DIM
    kcol = lax.broadcasted_iota(jnp.int32, (2 * KEY_BLOCK, 2 * KEY_BLOCK), 1)
    in_prev = kcol < KEY_BLOCK
    no_prev_first = jnp.where(t == 0, NEG, 0.0)

    @pl.when(jnp.logical_and(pl.program_id(1) == 0, t == 0))
    def _():
        for g in range(N_BRANCH):
            for hd in range(HEADS_PER_STEP):
                steps = jnp.broadcast_to(steps_ref[g, hd], (KEY_BLOCK, LANES))
                rolled = pltpu.roll(steps, 0, 1, stride=1, stride_axis=0)
                step0 = jnp.broadcast_to(step0_ref[g, hd], (KEY_BLOCK, LANES))
                left = jnp.where(lane >= qrow, rolled, NEG)
                right = jnp.where(lane < qrow, rolled, jnp.where(lane == qrow, step0, NEG))
                bias_sc[g, hd * KEY_BLOCK:(hd + 1) * KEY_BLOCK, :] = jnp.concatenate([left, right], axis=1)

    pooled, loops = [], []
    for g, d in enumerate(DILATIONS):
        q_ref, kc_ref, vc_ref, kp_ref, vp_ref = ins[5 * g:5 * g + 5]
        nqb = Q_TILE // d // KEY_BLOCK

        def rows(start, size, d=d):
            if d == 1:
                return pl.ds(start if isinstance(start, int) else pl.multiple_of(start, KEY_BLOCK), size)
            return pl.ds(start, size, stride=d)

        def load_first(r, q_ref=q_ref, kc_ref=kc_ref, vc_ref=vc_ref, kp_ref=kp_ref, vp_ref=vp_ref, rows=rows):
            sel = rows(r, KEY_BLOCK)
            ks = jnp.concatenate([kp_ref[sel, :], kc_ref[sel, :]], axis=0)
            vs = jnp.concatenate([vp_ref[sel, :], vc_ref[sel, :]], axis=0)
            return sel, q_ref[sel, :], ks, vs

        def load_later(sp, d=d, nqb=nqb, q_ref=q_ref, kc_ref=kc_ref, vc_ref=vc_ref, rows=rows):
            r = sp // (nqb - 1)
            qb = 1 + sp % (nqb - 1)
            q_sel = rows(d * KEY_BLOCK * qb + r, KEY_BLOCK)
            k_sel = rows(d * KEY_BLOCK * (qb - 1) + r, 2 * KEY_BLOCK)
            return q_sel, q_ref[q_sel, :], kc_ref[k_sel, :], vc_ref[k_sel, :]

        if d < max_group:
            pooled += [(g, load_first, r, no_prev_first) for r in range(d)]
        else:
            loops.append((g, d, load_first, no_prev_first))
        if nqb > 1:
            loops.append((g, d * (nqb - 1), load_later, None))

    def attend_group(items):
        loaded = [(g, no_prev) + load(idx) for g, load, idx, no_prev in items]
        done = [(g, sel, _attend_block(q2, ks, vs, bias_sc[g], no_prev, head_a, in_prev))
                for g, no_prev, sel, q2, ks, vs in loaded]
        for g, sel, (o2, l2) in done:
            oscs[g][sel, :] = o2
            lscs[g][sel, :] = l2

    for i in range(0, len(pooled), max_group):
        attend_group(pooled[i:i + max_group])
    for g, n_blocks, load, no_prev in loops:
        group = max(u for u in range(1, max_group + 1) if n_blocks % u == 0)

        def body(it, carry, g=g, load=load, no_prev=no_prev, group=group):
            attend_group([(g, load, it * group + u, no_prev) for u in range(group)])
            return carry

        lax.fori_loop(0, n_blocks // group, body, 0)

    merge_rows = 256
    for c in range(Q_TILE // merge_rows):
        rs = slice(c * merge_rows, (c + 1) * merge_rows)
        ls = [lscs[g][rs, :] for g in range(N_BRANCH)]
        m = jnp.maximum(jnp.maximum(ls[0], ls[1]), ls[2])
        es = [jnp.exp(l - m) for l in ls]
        num = es[0] * oscs[0][rs, :] + es[1] * oscs[1][rs, :] + es[2] * oscs[2][rs, :]
        o_ref[rs, :] = (num / (es[0] + es[1] + es[2])).astype(o_ref.dtype)


def _prompt_attention(q, kv, steps_rev, step0, *, batch, seq, max_group=ATTN_GROUP):
    n_tiles = seq // Q_TILE
    in_specs = []
    for g, d in enumerate(DILATIONS):
        prev_rows = KEY_BLOCK * d
        per_tile = Q_TILE // prev_rows
        q_col = g * N_PAIRS
        k_col = 2 * g * N_PAIRS
        v_col = k_col + N_PAIRS

        def cur(col):
            return pl.BlockSpec((Q_TILE, LANES), lambda hp, b, t, col=col: (b * n_tiles + t, col + hp))

        def prev(col, per_tile=per_tile, prev_rows=prev_rows):
            return pl.BlockSpec(
                (prev_rows, LANES),
                lambda hp, b, t, col=col: (jnp.maximum((b * n_tiles + t) * per_tile - 1, 0), col + hp))

        in_specs += [cur(q_col), cur(k_col), cur(v_col), prev(k_col), prev(v_col)]
    head_vec = pl.BlockSpec((N_BRANCH, HEADS_PER_STEP, 1, LANES), lambda hp, b, t: (0, hp, 0, 0))
    in_specs += [head_vec, head_vec]
    return pl.pallas_call(
        functools.partial(_prompt_attn_kernel, max_group=max_group),
        grid=(N_PAIRS, batch, n_tiles),
        in_specs=in_specs,
        out_specs=pl.BlockSpec((Q_TILE, LANES), lambda hp, b, t: (b * n_tiles + t, hp)),
        out_shape=jax.ShapeDtypeStruct((batch * seq, D_MODEL), BF16),
        scratch_shapes=[pltpu.VMEM((Q_TILE, LANES), F32)] * (2 * N_BRANCH)
        + [pltpu.VMEM((N_BRANCH, 2 * KEY_BLOCK, 2 * KEY_BLOCK), F32)],
        compiler_params=_params(3), name="prompt_attention",
    )(*([q, kv, kv, kv, kv] * N_BRANCH), steps_rev, step0)


def _sample_attn_kernel(*refs, n_new, heads):
    qkv_refs, caches, cbias, bn_ref, o_ref = refs[:9], refs[9:12], refs[12:15], refs[15], refs[16]
    j = pl.program_id(1)
    which = pl.program_id(2)
    width = heads * HEAD_DIM
    per_block = NEW_ROWS // n_new

    def own_rows(ref):
        rows = ref[0:n_new, :]
        for w in range(1, per_block):
            rows = jnp.where(which == w, ref[w * n_new:(w + 1) * n_new, :], rows)
        return rows

    pad = jnp.zeros((NEW_ROWS - n_new, width), F32)
    lane_head = lax.broadcasted_iota(jnp.int32, (heads, width), 1) // HEAD_DIM
    own_lanes = lane_head == lax.broadcasted_iota(jnp.int32, (heads, width), 0)

    def per_head_rows(x):
        return jnp.concatenate(
            [jnp.sum(jnp.where(own_lanes, x[i * heads:(i + 1) * heads], 0.0), axis=0, keepdims=True)
             for i in range(n_new)], axis=0)

    outs, lses = [], []
    for g in range(N_BRANCH):
        q = own_rows(qkv_refs[3 * g]) * Q_SCALE
        k_new = jnp.concatenate([own_rows(qkv_refs[3 * g + 1]), pad], axis=0).astype(BF16)
        v_new = jnp.concatenate([own_rows(qkv_refs[3 * g + 2]), pad], axis=0).astype(BF16)
        qbd = jnp.concatenate([jnp.where(own_lanes, q[i:i + 1, :], 0.0) for i in range(n_new)],
                              axis=0).astype(BF16)
        n_pos = caches[g].shape[-1]
        k_t = caches[g][0, 0].reshape(width, n_pos).astype(BF16)
        v_t = caches[g][0, 1].reshape(width, n_pos).astype(BF16)
        s_c = _dot(qbd, k_t) + cbias[g][j]
        s_n = _dot_nt(qbd, k_new) + bn_ref[g, j]
        m = jnp.maximum(jnp.max(s_c, axis=-1, keepdims=True), jnp.max(s_n, axis=-1, keepdims=True))
        p_c = jnp.exp(s_c - m)
        p_n = jnp.exp(s_n - m)
        den = jnp.sum(p_c, axis=-1, keepdims=True) + jnp.sum(p_n, axis=-1, keepdims=True)
        o = (_dot_nt(p_c.astype(BF16), v_t) + _dot(p_n.astype(BF16), v_new)) / den
        outs.append(per_head_rows(o))
        lses.append(per_head_rows(jnp.broadcast_to(m + jnp.log(den), o.shape)))
    m = jnp.maximum(jnp.maximum(lses[0], lses[1]), lses[2])
    es = [jnp.exp(l - m) for l in lses]
    num = es[0] * outs[0] + es[1] * outs[1] + es[2] * outs[2]
    merged = num / (es[0] + es[1] + es[2])
    for w in range(per_block):
        @pl.when(which == w)
        def _(w=w):
            o_ref[w * n_new:(w + 1) * n_new, :] = merged


def _sample_attention(q, kv, caches, cache_bias, new_bias, *, n_new, heads_per_step=N_HEADS):
    hs = heads_per_step
    width = hs * HEAD_DIM
    per_block = NEW_ROWS // n_new
    n_blocks = q.shape[0] // NEW_ROWS
    window = lambda col: pl.BlockSpec((NEW_ROWS, width), lambda r, j, w, col=col: (r, col + j))
    per_d = D_MODEL // width
    windows = []
    for g in range(N_BRANCH):
        windows += [window(g * per_d), window(2 * g * per_d), window((2 * g + 1) * per_d)]
    return pl.pallas_call(
        functools.partial(_sample_attn_kernel, n_new=n_new, heads=hs),
        grid=(n_blocks, N_HEADS // hs, per_block),
        in_specs=windows
        + [pl.BlockSpec((1, 2, hs, HEAD_DIM, c.shape[-1]), lambda r, j, w: (r * per_block + w, 0, j, 0, 0))
           for c in caches]
        + [_const_spec(t.shape) for t in cache_bias] + [_const_spec(new_bias.shape)],
        out_specs=pl.BlockSpec((NEW_ROWS, width), lambda r, j, w: (r, j)),
        out_shape=jax.ShapeDtypeStruct((q.shape[0], D_MODEL), F32),
        compiler_params=_params(3, SAMPLE_ATTN_VMEM_LIMIT_BYTES), name="sample_attention",
    )(*([q, kv, kv] * N_BRANCH), *caches, *cache_bias, new_bias)


def _t5_bucket(dist):
    max_exact = NUM_BUCKETS // 2
    n = jnp.maximum(dist, 1).astype(F32)
    large = max_exact + (jnp.log(n / max_exact) / math.log(MAX_DISTANCE / max_exact)
                         * (NUM_BUCKETS - max_exact)).astype(jnp.int32)
    large = jnp.minimum(large, NUM_BUCKETS - 1)
    return jnp.where(dist < max_exact, dist, large)


def _branch_bias(rel_bias, g):
    n_keys = WINDOWS[g] // DILATIONS[g] + 1
    dist = jnp.arange(n_keys, dtype=jnp.int32) * DILATIONS[g]
    return rel_bias[_t5_bucket(dist)][:, g * N_HEADS:(g + 1) * N_HEADS].T.astype(F32)


def _sample_bias_tables(biases, n_new, hs):
    neg = lambda *shape: jnp.full(shape, NEG, F32)
    by_step = lambda t: t.reshape(N_HEADS // hs, hs, n_new, -1).transpose(0, 2, 1, 3).reshape(
        N_HEADS // hs, n_new * hs, -1)
    cache_tabs, new_tabs = [], []
    for g, d in enumerate(DILATIONS):
        w = WINDOWS[g]
        n_steps = w // d
        b = biases[g]
        rows, new_rows = [], []
        for i in range(n_new):
            first = n_steps + i // d
            hit = b[:, min(first, n_steps):i // d:-1]
            hit = jnp.concatenate([neg(N_HEADS, n_steps - hit.shape[1]), hit], axis=1)
            grid = jnp.where((jnp.arange(d) == i % d)[None, None, :], hit[:, :, None], NEG)
            rows.append(grid.reshape(N_HEADS, w))
            cols = [b[:, (i - j) // d] if (j <= i and (i - j) % d == 0 and j < n_new) else neg(N_HEADS)
                    for j in range(NEW_ROWS)]
            new_rows.append(jnp.stack(cols, axis=1))
        cache_tabs.append(by_step(jnp.stack(rows, axis=1)))
        new_tabs.append(by_step(jnp.stack(new_rows, axis=1)))
    return cache_tabs, jnp.stack(new_tabs)


def kernel(x_prompt, x_sample, cache_kv_w128, cache_kv_w512, cache_kv_w2048, ln_g, ln_b, gm_w_in, gm_b_in,
           gm_ln_g, gm_ln_b, gm_w_s, gm_b_s, gm_w_out, gm_b_out, w_kv, attn_w_q, attn_w_o, rel_bias,
           mlp_w1, mlp_w2):
    batch, seq, _ = x_prompt.shape
    n_samp, n_new, _ = x_sample.shape
    ms = n_samp * n_new
    assert DEPTH == 2 and gm_w_in.shape[0] == 1 and attn_w_q.shape[0] == 1
    assert seq % Q_TILE == 0 and ms % CHUNK == 0 and CHUNK % n_new == 0 and NEW_ROWS % n_new == 0
    assert all(PAST_LEN >= w and w % d == 0 and w // d == KEY_BLOCK for w, d in zip(WINDOWS, DILATIONS))

    bf = lambda w: w.astype(BF16)
    vec = lambda v: v.reshape(1, -1)
    w_in, w_out = bf(gm_w_in[0]), bf(gm_w_out[0])
    wq_t = jnp.transpose(attn_w_q[0], (1, 2, 3, 0)).reshape(N_BRANCH * D_MODEL, D_MODEL)
    wkv_t = jnp.transpose(w_kv, (1, 2, 3, 4, 0)).reshape(N_BRANCH * 2 * D_MODEL, D_MODEL)
    gm_args = (vec(gm_b_in[0]), vec(gm_ln_g[0]), vec(gm_ln_b[0]))
    gm_tail = (w_out, vec(gm_b_out[0]), vec(ln_g[0, 0]), vec(ln_b[0, 0]))

    xp = x_prompt.reshape(batch * seq, D_MODEL)
    xs = x_sample.reshape(ms, D_MODEL)
    x1p, vn_p = _gmlp_layer(xp, w_in, *gm_args, gm_w_s[0], gm_b_s[0].T, *gm_tail, tm=GMLP_ROWS, rows_per_seq=seq)
    reps = CHUNK // n_new
    same_seq = (jnp.arange(CHUNK)[:, None] // n_new) == (jnp.arange(CHUNK)[None, :] // n_new)
    ws_s = jnp.where(same_seq[None], jnp.tile(gm_w_s[0][:, :n_new, :n_new], (1, reps, reps)), 0.0)
    bs_s = jnp.tile(gm_b_s[0][:, :n_new], (1, reps)).T
    x1s, vn_s = _gmlp_layer(xs, w_in, *gm_args, ws_s, bs_s, *gm_tail, tm=ms, rows_per_seq=None)

    x2s, w1b, w2b = _mlp_layer_rounding(x1s, mlp_w1, mlp_w2, 0, vec(ln_g[0, 1]), vec(ln_b[0, 1]))
    x2p = _mlp_layer(x1p, w1b, w2b, vec(ln_g[0, 1]), vec(ln_b[0, 1]), tm=MLP_ROWS)

    q_s, wq_tb = _project_rounding(x2s, wq_t, tn=ROUNDING_COLS)
    kv_s, wkv_tb = _project_rounding(x2s, wkv_t, tn=ROUNDING_COLS)
    q_p = _project(x2p, wq_tb, tm=PROJ_ROWS, tn=PROJ_COLS)
    kv_p = _project(x2p, wkv_tb, tm=PROJ_ROWS, tn=PROJ_COLS)

    biases = [_branch_bias(rel_bias, g) for g in range(N_BRANCH)]
    steps_rev = jnp.stack([b[:, :0:-1] for b in biases])[:, :, None, :]
    step0 = jnp.stack([jnp.broadcast_to(b[:, :1], (N_HEADS, LANES)) for b in biases])[:, :, None, :]
    o_p = _prompt_attention(q_p, kv_p, steps_rev, step0, batch=batch, seq=seq)

    caches = [jnp.transpose(c, (0, 2, 3, 4, 1)) for c in (cache_kv_w128, cache_kv_w512, cache_kv_w2048)]
    cache_bias, new_bias = _sample_bias_tables(biases, n_new, N_HEADS)
    o_s = _sample_attention(q_s, kv_s, caches, cache_bias, new_bias, n_new=n_new)

    attn_ln = (vec(ln_g[1, 0]), vec(ln_b[1, 0]))
    y_s, w1b, w2b, wob = _mlp_layer_rounding(x2s, mlp_w1, mlp_w2, 1, vec(ln_g[1, 1]), vec(ln_b[1, 1]),
                                             attn=(o_s, attn_w_o[0]) + attn_ln)
    y_p = _mlp_layer(x2p, w1b, w2b, vec(ln_g[1, 1]), vec(ln_b[1, 1]), tm=MLP_ROWS, attn=(o_p, wob) + attn_ln)

    wkv_t3 = wkv_tb.reshape(N_BRANCH, 2 * D_MODEL, D_MODEL)
    kv_p_out = []
    for g in range(N_BRANCH):
        rows = min(WINDOWS[g], seq)
        kv_t = _kv_state(x2p, wkv_t3, g, batch=batch, seq=seq, rows=rows, tw=min(rows, KV_STATE_COLS))
        kv_p_out.append(kv_t.reshape(batch, 2, N_HEADS, HEAD_DIM, rows).transpose(0, 4, 1, 2, 3))
    kv_s_t = _kv_state_sample(x2s.reshape(n_samp, n_new, D_MODEL).transpose(1, 0, 2), wkv_t3)
    kv_s_out = [kv_s_t[g].reshape(n_new, 2, N_HEADS, HEAD_DIM, n_samp).transpose(4, 0, 1, 2, 3)
                for g in range(N_BRANCH)]
    return (y_p.reshape(batch, seq, D_MODEL), y_s.reshape(n_samp, n_new, D_MODEL),
            vn_p[None], vn_s.reshape(1, n_samp, n_new, GATE),
            kv_p_out[0], kv_s_out[0], kv_p_out[1], kv_s_out[1], kv_p_out[2], kv_s_out[2])
```

```python
import functools
import math

import jax
import jax.numpy as jnp
from jax import lax
from jax.experimental import pallas as pl
from jax.experimental.pallas import tpu as pltpu

D_MODEL = 1024
CHUNK = 128
GATE = 2 * D_MODEL
SGU_GROUPS = 8
SGU_GROUP_DIM = GATE // SGU_GROUPS
WINDOWS = (128, 512, 2048)
DILATIONS = (1, 4, 16)
N_BRANCH = 3
HEAD_DIM = 64
N_HEADS = 16
D_FF = 4 * D_MODEL
NUM_BUCKETS = 32
MAX_DISTANCE = 2048
PAST_LEN = 2048
DEPTH = 2
ALPHA = (2 * DEPTH) ** 0.25
LN_EPS = 1e-5
NEG = -1e30
Q_SCALE = HEAD_DIM ** -0.5

BF16 = jnp.bfloat16
F32 = jnp.float32

MIB = 1024 * 1024
V7X_VMEM_BYTES = 64 * MIB
VMEM_LIMIT_BYTES = V7X_VMEM_BYTES - 8 * MIB
SAMPLE_ATTN_VMEM_LIMIT_BYTES = V7X_VMEM_BYTES - 4 * MIB
LANES = 128
Q_TILE = 2048
KEY_BLOCK = 128
HEADS_PER_STEP = LANES // HEAD_DIM
N_PAIRS = N_HEADS // HEADS_PER_STEP
NEW_ROWS = 8

GMLP_ROWS = 512
MLP_ROWS = 512
PROJ_ROWS = 512
ROUNDING_COLS = D_MODEL
KV_STATE_COLS = 1024
ATTN_GROUP = 6


def _params(n_axes, vmem_limit_bytes=VMEM_LIMIT_BYTES):
    return pltpu.CompilerParams(dimension_semantics=("arbitrary",) * n_axes, vmem_limit_bytes=vmem_limit_bytes)


def _const_spec(shape):
    return pl.BlockSpec(shape, lambda *_: (0,) * len(shape), pipeline_mode=pl.Buffered(1))


def _ln(x, g, b):
    mu = jnp.mean(x, axis=-1, keepdims=True)
    xc = x - mu
    var = jnp.mean(xc * xc, axis=-1, keepdims=True)
    return xc * lax.rsqrt(var + LN_EPS) * g + b


def _dot(a, b):
    return jnp.dot(a, b, preferred_element_type=F32)


def _dot_nt(a, b):
    return lax.dot_general(a, b, (((1,), (1,)), ((), ())), preferred_element_type=F32)


def _gmlp_kernel(x_ref, w_in_ref, b_in_ref, gv_ref, bv_ref, ws_ref, bs_ref, w_out_ref, b_out_ref,
                 g_ref, b_ref, x1_ref, vn_ref, vnb_sc, gact_sc, *, tm, blocks_per_seq):
    x = x_ref[...]
    xb = x.astype(BF16)
    zv = _dot(xb, w_in_ref[:, GATE:]) + b_in_ref[:, GATE:]
    vn = _ln(jax.nn.gelu(zv), gv_ref[...], bv_ref[...])
    vnb_sc[...] = vn.astype(BF16)

    if blocks_per_seq is None:
        vn_ref[...] = vn
    else:
        @pl.when(pl.program_id(0) % blocks_per_seq == blocks_per_seq - 1)
        def _():
            vn_ref[0] = vn[tm - CHUNK:, :]

    row = lax.broadcasted_iota(jnp.int32, (CHUNK, CHUNK), 0)
    col = lax.broadcasted_iota(jnp.int32, (CHUNK, CHUNK), 1)
    causal = row >= col
    for g in range(SGU_GROUPS):
        lo, hi = g * SGU_GROUP_DIM, (g + 1) * SGU_GROUP_DIM
        zu = jax.nn.gelu(_dot(xb, w_in_ref[:, lo:hi]) + b_in_ref[:, lo:hi])
        wg = jnp.where(causal, ws_ref[g], 0.0).astype(BF16)
        bsg = bs_ref[:, g:g + 1]
        for c in range(tm // CHUNK):
            r0, r1 = c * CHUNK, (c + 1) * CHUNK
            s = _dot(wg, vnb_sc[r0:r1, lo:hi]) + bsg
            gact_sc[r0:r1, lo:hi] = (zu[r0:r1] * s).astype(BF16)
    y = _dot(gact_sc[...], w_out_ref[...]) + b_out_ref[...]
    x1_ref[...] = _ln(ALPHA * x + y, g_ref[...], b_ref[...])


def _gmlp_layer(x, w_in, b_in, gv, bv, ws, bs_t, w_out, b_out, g, b, *, tm, rows_per_seq):
    m = x.shape[0]
    if rows_per_seq is None:
        blocks_per_seq = None
        vn_shape = jax.ShapeDtypeStruct((m, GATE), F32)
        vn_spec = pl.BlockSpec((tm, GATE), lambda i: (i, 0))
    else:
        blocks_per_seq = rows_per_seq // tm
        vn_shape = jax.ShapeDtypeStruct((m // rows_per_seq, CHUNK, GATE), F32)
        vn_spec = pl.BlockSpec((1, CHUNK, GATE), lambda i: (i // blocks_per_seq, 0, 0))
    return pl.pallas_call(
        functools.partial(_gmlp_kernel, tm=tm, blocks_per_seq=blocks_per_seq),
        grid=(m // tm,),
        in_specs=[
            pl.BlockSpec((tm, D_MODEL), lambda i: (i, 0)),
            _const_spec((D_MODEL, 2 * GATE)), _const_spec((1, 2 * GATE)),
            _const_spec((1, GATE)), _const_spec((1, GATE)),
            _const_spec((SGU_GROUPS, CHUNK, CHUNK)), _const_spec((CHUNK, SGU_GROUPS)),
            _const_spec((GATE, D_MODEL)), _const_spec((1, D_MODEL)),
            _const_spec((1, D_MODEL)), _const_spec((1, D_MODEL)),
        ],
        out_specs=[pl.BlockSpec((tm, D_MODEL), lambda i: (i, 0)), vn_spec],
        out_shape=[jax.ShapeDtypeStruct((m, D_MODEL), F32), vn_shape],
        scratch_shapes=[pltpu.VMEM((tm, GATE), BF16), pltpu.VMEM((tm, GATE), BF16)],
        compiler_params=_params(1),
        name="gmlp_layer",
    )(x, w_in, b_in, gv, bv, ws, bs_t, w_out, b_out, g, b)


def _mlp_body(x, w1_ref, w2_ref, g_ref, b_ref, out_ref):
    xb = x.astype(BF16)
    acc = jnp.zeros_like(x)
    for j in range(D_FF // D_MODEL):
        lo, hi = j * D_MODEL, (j + 1) * D_MODEL
        h = jnp.square(jnp.maximum(_dot(xb, w1_ref[:, lo:hi]), 0.0)).astype(BF16)
        acc = acc + _dot(h, w2_ref[lo:hi, :])
    out_ref[...] = _ln(ALPHA * x + acc, g_ref[...], b_ref[...])


def _mlp_kernel(x_ref, w1_ref, w2_ref, g_ref, b_ref, out_ref):
    _mlp_body(x_ref[...], w1_ref, w2_ref, g_ref, b_ref, out_ref)


def _attn_mlp_kernel(x_ref, o_ref, wo_ref, ga_ref, ba_ref, w1_ref, w2_ref, g_ref, b_ref, out_ref):
    x = _ln(ALPHA * x_ref[...] + _dot(o_ref[...].astype(BF16), wo_ref[...]), ga_ref[...], ba_ref[...])
    _mlp_body(x, w1_ref, w2_ref, g_ref, b_ref, out_ref)


def _mlp_rounding_kernel(*refs, has_attn):
    if has_attn:
        (x_ref, o_ref, wo_ref, ga_ref, ba_ref, w1_ref, w2_ref, g_ref, b_ref,
         out_ref, w1b_ref, w2b_ref, wob_ref, x_sc, xb_sc, acc_sc) = refs
    else:
        x_ref, w1_ref, w2_ref, g_ref, b_ref, out_ref, w1b_ref, w2b_ref, x_sc, xb_sc, acc_sc = refs
    j = pl.program_id(0)

    @pl.when(j == 0)
    def _():
        x = x_ref[...]
        if has_attn:
            wob = wo_ref[...].astype(BF16)
            wob_ref[...] = wob
            x = _ln(ALPHA * x + _dot(o_ref[...].astype(BF16), wob), ga_ref[...], ba_ref[...])
        x_sc[...] = x
        xb_sc[...] = x.astype(BF16)
        acc_sc[...] = jnp.zeros_like(acc_sc)

    w1b = w1_ref[...].astype(BF16)
    w2b = w2_ref[...].astype(BF16)
    w1b_ref[...] = w1b
    w2b_ref[...] = w2b
    h = jnp.square(jnp.maximum(_dot(xb_sc[...], w1b), 0.0)).astype(BF16)
    acc_sc[...] = acc_sc[...] + _dot(h, w2b)

    @pl.when(j == pl.num_programs(0) - 1)
    def _():
        out_ref[...] = _ln(ALPHA * x_sc[...] + acc_sc[...], g_ref[...], b_ref[...])


def _mlp_layer_rounding(x, w1, w2, layer, g, b, *, attn=None):
    m = x.shape[0]
    chunk = ROUNDING_COLS
    rows = _const_spec((m, D_MODEL))
    vec = _const_spec((1, D_MODEL))
    w_specs = [pl.BlockSpec((None, D_MODEL, chunk), lambda j: (layer, 0, j)),
               pl.BlockSpec((None, chunk, D_MODEL), lambda j: (layer, j, 0)), vec, vec]
    whole = lambda *shape: pl.BlockSpec(shape, lambda j: (0,) * len(shape))
    out_specs = [whole(m, D_MODEL), pl.BlockSpec((D_MODEL, chunk), lambda j: (0, j)),
                 pl.BlockSpec((chunk, D_MODEL), lambda j: (j, 0))]
    out_shape = [jax.ShapeDtypeStruct((m, D_MODEL), F32), jax.ShapeDtypeStruct((D_MODEL, D_FF), BF16),
                 jax.ShapeDtypeStruct((D_FF, D_MODEL), BF16)]
    if attn is None:
        args, specs = (x, w1, w2, g, b), [rows] + w_specs
    else:
        o, wo, ga, ba = attn
        args = (x, o, wo, ga, ba, w1, w2, g, b)
        specs = [rows, rows, _const_spec((D_MODEL, D_MODEL)), vec, vec] + w_specs
        out_specs.append(whole(D_MODEL, D_MODEL))
        out_shape.append(jax.ShapeDtypeStruct((D_MODEL, D_MODEL), BF16))
    return pl.pallas_call(
        functools.partial(_mlp_rounding_kernel, has_attn=attn is not None),
        grid=(D_FF // chunk,), in_specs=specs, out_specs=out_specs, out_shape=out_shape,
        scratch_shapes=[pltpu.VMEM((m, D_MODEL), F32), pltpu.VMEM((m, D_MODEL), BF16), pltpu.VMEM((m, D_MODEL), F32)],
        compiler_params=_params(1), name="mlp_layer_rounding",
    )(*args)


def _mlp_layer(x, w1, w2, g, b, *, tm, attn=None):
    m = x.shape[0]
    row_spec = pl.BlockSpec((tm, D_MODEL), lambda i: (i, 0))
    vec = _const_spec((1, D_MODEL))
    mlp_specs = [_const_spec((D_MODEL, D_FF)), _const_spec((D_FF, D_MODEL)), vec, vec]
    if attn is None:
        kern, args, specs = _mlp_kernel, (x, w1, w2, g, b), [row_spec] + mlp_specs
    else:
        o, wo, ga, ba = attn
        kern = _attn_mlp_kernel
        args = (x, o, wo, ga, ba, w1, w2, g, b)
        specs = [row_spec, row_spec, _const_spec((D_MODEL, D_MODEL)), vec, vec] + mlp_specs
    return pl.pallas_call(
        kern, grid=(m // tm,), in_specs=specs, out_specs=row_spec,
        out_shape=jax.ShapeDtypeStruct((m, D_MODEL), F32),
        compiler_params=_params(1), name="mlp_layer",
    )(*args)


def _project_kernel(x_ref, wt_ref, o_ref):
    o_ref[...] = _dot_nt(x_ref[...].astype(BF16), wt_ref[...])


def _project(x, w_t, *, tm):
    m, k = x.shape
    n = w_t.shape[0]
    return pl.pallas_call(
        _project_kernel, grid=(m // tm,),
        in_specs=[pl.BlockSpec((tm, k), lambda i: (i, 0)), _const_spec((n, k))],
        out_specs=pl.BlockSpec((tm, n), lambda i: (i, 0)),
        out_shape=jax.ShapeDtypeStruct((m, n), F32),
        compiler_params=_params(1), name="qkv_projection",
    )(x, w_t)


def _project_rounding_kernel(x_ref, wt_ref, o_ref, wtb_ref):
    wtb = wt_ref[...].astype(BF16)
    wtb_ref[...] = wtb
    o_ref[...] = _dot_nt(x_ref[...].astype(BF16), wtb)


def _project_rounding(x, w_t, *, tn):
    m, k = x.shape
    n = w_t.shape[0]
    return pl.pallas_call(
        _project_rounding_kernel, grid=(n // tn,),
        in_specs=[_const_spec((m, k)), pl.BlockSpec((tn, k), lambda j: (j, 0))],
        out_specs=[pl.BlockSpec((m, tn), lambda j: (0, j)), pl.BlockSpec((tn, k), lambda j: (j, 0))],
        out_shape=[jax.ShapeDtypeStruct((m, n), F32), jax.ShapeDtypeStruct((n, k), BF16)],
        compiler_params=_params(1), name="qkv_projection_rounding",
    )(x, w_t)


def _kv_state_kernel(x_ref, wt_ref, o_ref):
    o_ref[...] = _dot_nt(wt_ref[...], x_ref[...].astype(BF16))


def _kv_state(x, wkv_t, g, *, batch, seq, rows, tw):
    per_seq = seq // tw
    first = (seq - rows) // tw
    return pl.pallas_call(
        _kv_state_kernel, grid=(batch, rows // tw),
        in_specs=[pl.BlockSpec((tw, D_MODEL), lambda b, j: (b * per_seq + first + j, 0)),
                  pl.BlockSpec((None, 2 * D_MODEL, D_MODEL), lambda b, j: (g, 0, 0), pipeline_mode=pl.Buffered(1))],
        out_specs=pl.BlockSpec((None, 2 * D_MODEL, tw), lambda b, j: (b, 0, j)),
        out_shape=jax.ShapeDtypeStruct((batch, 2 * D_MODEL, rows), F32),
        compiler_params=_params(2), name="kv_state",
    )(x, wkv_t)


def _kv_state_sample(x3, wkv_t):
    n_new, nb, _ = x3.shape
    return pl.pallas_call(
        _kv_state_kernel, grid=(N_BRANCH, n_new),
        in_specs=[pl.BlockSpec((None, nb, D_MODEL), lambda g, i: (i, 0, 0)),
                  pl.BlockSpec((None, 2 * D_MODEL, D_MODEL), lambda g, i: (g, 0, 0))],
        out_specs=pl.BlockSpec((None, None, 2 * D_MODEL, nb), lambda g, i: (g, i, 0, 0)),
        out_shape=jax.ShapeDtypeStruct((N_BRANCH, n_new, 2 * D_MODEL, nb), F32),
        compiler_params=_params(2), name="kv_state_sample",
    )(x3, wkv_t)


def _attend_block(q2, ks, vs, bias, no_prev, head_a, in_prev):
    qb = (q2 * Q_SCALE).astype(BF16)
    zero = jnp.zeros_like(qb)
    qs = jnp.concatenate([jnp.where(head_a, qb, zero), jnp.where(head_a, zero, qb)], axis=0)
    s = _dot_nt(qs, ks.astype(BF16)) + bias
    if no_prev is not None:
        s = s + jnp.where(in_prev, no_prev, 0.0)
    m = jnp.max(s, axis=-1, keepdims=True)
    p = jnp.exp(s - m)
    vs_aug = jnp.concatenate([vs.astype(BF16), jnp.ones((2 * KEY_BLOCK, LANES), BF16)], axis=1)
    ov = _dot(p.astype(BF16), vs_aug)
    num2 = jnp.where(head_a, ov[:KEY_BLOCK, :LANES], ov[KEY_BLOCK:, :LANES])
    den2 = jnp.where(head_a, ov[:KEY_BLOCK, LANES:], ov[KEY_BLOCK:, LANES:])
    l2 = jnp.where(head_a, m[:KEY_BLOCK], m[KEY_BLOCK:]) + jnp.log(den2)
    return num2 / den2, l2


def _prompt_attn_kernel(*refs, max_group):
    ins, (steps_ref, step0_ref, o_ref), scr = refs[:15], refs[15:18], refs[18:]
    oscs, lscs, bias_sc = scr[0:3], scr[3:6], scr[6]
    t = pl.program_id(2)
    lane = lax.broadcasted_iota(jnp.int32, (KEY_BLOCK, LANES), 1)
    qrow = lax.broadcasted_iota(jnp.int32, (KEY_BLOCK, LANES), 0)
    head_a = lane < HEAD_DIM
    kcol = lax.broadcasted_iota(jnp.int32, (2 * KEY_BLOCK, 2 * KEY_BLOCK), 1)
    in_prev = kcol < KEY_BLOCK
    no_prev_first = jnp.where(t == 0, NEG, 0.0)

    @pl.when(jnp.logical_and(pl.program_id(1) == 0, t == 0))
    def _():
        for g in range(N_BRANCH):
            for hd in range(HEADS_PER_STEP):
                steps = jnp.broadcast_to(steps_ref[g, hd], (KEY_BLOCK, LANES))
                rolled = pltpu.roll(steps, 0, 1, stride=1, stride_axis=0)
                step0 = jnp.broadcast_to(step0_ref[g, hd], (KEY_BLOCK, LANES))
                left = jnp.where(lane >= qrow, rolled, NEG)
                right = jnp.where(lane < qrow, rolled, jnp.where(lane == qrow, step0, NEG))
                bias_sc[g, hd * KEY_BLOCK:(hd + 1) * KEY_BLOCK, :] = jnp.concatenate([left, right], axis=1)

    pooled, loops = [], []
    for g, d in enumerate(DILATIONS):
        q_ref, kc_ref, vc_ref, kp_ref, vp_ref = ins[5 * g:5 * g + 5]
        nqb = Q_TILE // d // KEY_BLOCK

        def rows(start, size, d=d):
            if d == 1:
                return pl.ds(start if isinstance(start, int) else pl.multiple_of(start, KEY_BLOCK), size)
            return pl.ds(start, size, stride=d)

        def load_first(r, q_ref=q_ref, kc_ref=kc_ref, vc_ref=vc_ref, kp_ref=kp_ref, vp_ref=vp_ref, rows=rows):
            sel = rows(r, KEY_BLOCK)
            ks = jnp.concatenate([kp_ref[sel, :], kc_ref[sel, :]], axis=0)
            vs = jnp.concatenate([vp_ref[sel, :], vc_ref[sel, :]], axis=0)
            return sel, q_ref[sel, :], ks, vs

        def load_later(sp, d=d, nqb=nqb, q_ref=q_ref, kc_ref=kc_ref, vc_ref=vc_ref, rows=rows):
            r = sp // (nqb - 1)
            qb = 1 + sp % (nqb - 1)
            q_sel = rows(d * KEY_BLOCK * qb + r, KEY_BLOCK)
            k_sel = rows(d * KEY_BLOCK * (qb - 1) + r, 2 * KEY_BLOCK)
            return q_sel, q_ref[q_sel, :], kc_ref[k_sel, :], vc_ref[k_sel, :]

        if d < max_group:
            pooled += [(g, load_first, r, no_prev_first) for r in range(d)]
        else:
            loops.append((g, d, load_first, no_prev_first))
        if nqb > 1:
            loops.append((g, d * (nqb - 1), load_later, None))

    def attend_group(items):
        loaded = [(g, no_prev) + load(idx) for g, load, idx, no_prev in items]
        done = [(g, sel, _attend_block(q2, ks, vs, bias_sc[g], no_prev, head_a, in_prev))
                for g, no_prev, sel, q2, ks, vs in loaded]
        for g, sel, (o2, l2) in done:
            oscs[g][sel, :] = o2
            lscs[g][sel, :] = l2

    for i in range(0, len(pooled), max_group):
        attend_group(pooled[i:i + max_group])
    for g, n_blocks, load, no_prev in loops:
        group = max(u for u in range(1, max_group + 1) if n_blocks % u == 0)

        def body(it, carry, g=g, load=load, no_prev=no_prev, group=group):
            attend_group([(g, load, it * group + u, no_prev) for u in range(group)])
            return carry

        lax.fori_loop(0, n_blocks // group, body, 0)

    merge_rows = 256
    for c in range(Q_TILE // merge_rows):
        rs = slice(c * merge_rows, (c + 1) * merge_rows)
        ls = [lscs[g][rs, :] for g in range(N_BRANCH)]
        m = jnp.maximum(jnp.maximum(ls[0], ls[1]), ls[2])
        es = [jnp.exp(l - m) for l in ls]
        num = es[0] * oscs[0][rs, :] + es[1] * oscs[1][rs, :] + es[2] * oscs[2][rs, :]
        o_ref[rs, :] = (num / (es[0] + es[1] + es[2])).astype(o_ref.dtype)


def _prompt_attention(q, kv, steps_rev, step0, *, batch, seq, max_group=ATTN_GROUP):
    n_tiles = seq // Q_TILE
    in_specs = []
    for g, d in enumerate(DILATIONS):
        prev_rows = KEY_BLOCK * d
        per_tile = Q_TILE // prev_rows
        q_col = g * N_PAIRS
        k_col = 2 * g * N_PAIRS
        v_col = k_col + N_PAIRS

        def cur(col):
            return pl.BlockSpec((Q_TILE, LANES), lambda hp, b, t, col=col: (b * n_tiles + t, col + hp))

        def prev(col, per_tile=per_tile, prev_rows=prev_rows):
            return pl.BlockSpec(
                (prev_rows, LANES),
                lambda hp, b, t, col=col: (jnp.maximum((b * n_tiles + t) * per_tile - 1, 0), col + hp))

        in_specs += [cur(q_col), cur(k_col), cur(v_col), prev(k_col), prev(v_col)]
    head_vec = pl.BlockSpec((N_BRANCH, HEADS_PER_STEP, 1, LANES), lambda hp, b, t: (0, hp, 0, 0))
    in_specs += [head_vec, head_vec]
    return pl.pallas_call(
        functools.partial(_prompt_attn_kernel, max_group=max_group),
        grid=(N_PAIRS, batch, n_tiles),
        in_specs=in_specs,
        out_specs=pl.BlockSpec((Q_TILE, LANES), lambda hp, b, t: (b * n_tiles + t, hp)),
        out_shape=jax.ShapeDtypeStruct((batch * seq, D_MODEL), BF16),
        scratch_shapes=[pltpu.VMEM((Q_TILE, LANES), F32)] * (2 * N_BRANCH)
        + [pltpu.VMEM((N_BRANCH, 2 * KEY_BLOCK, 2 * KEY_BLOCK), F32)],
        compiler_params=_params(3), name="prompt_attention",
    )(*([q, kv, kv, kv, kv] * N_BRANCH), steps_rev, step0)


def _sample_attn_kernel(*refs, n_new, heads):
    qkv_refs, caches, cbias, bn_ref, o_ref = refs[:9], refs[9:12], refs[12:15], refs[15], refs[16]
    j = pl.program_id(1)
    which = pl.program_id(2)
    width = heads * HEAD_DIM
    per_block = NEW_ROWS // n_new

    def own_rows(ref):
        rows = ref[0:n_new, :]
        for w in range(1, per_block):
            rows = jnp.where(which == w, ref[w * n_new:(w + 1) * n_new, :], rows)
        return rows

    pad = jnp.zeros((NEW_ROWS - n_new, width), F32)
    lane_head = lax.broadcasted_iota(jnp.int32, (heads, width), 1) // HEAD_DIM
    own_lanes = lane_head == lax.broadcasted_iota(jnp.int32, (heads, width), 0)

    def per_head_rows(x):
        return jnp.concatenate(
            [jnp.sum(jnp.where(own_lanes, x[i * heads:(i + 1) * heads], 0.0), axis=0, keepdims=True)
             for i in range(n_new)], axis=0)

    outs, lses = [], []
    for g in range(N_BRANCH):
        q = own_rows(qkv_refs[3 * g]) * Q_SCALE
        k_new = jnp.concatenate([own_rows(qkv_refs[3 * g + 1]), pad], axis=0).astype(BF16)
        v_new = jnp.concatenate([own_rows(qkv_refs[3 * g + 2]), pad], axis=0).astype(BF16)
        qbd = jnp.concatenate([jnp.where(own_lanes, q[i:i + 1, :], 0.0) for i in range(n_new)],
                              axis=0).astype(BF16)
        n_pos = caches[g].shape[-1]
        k_t = caches[g][0, 0].reshape(width, n_pos).astype(BF16)
        v_t = caches[g][0, 1].reshape(width, n_pos).astype(BF16)
        s_c = _dot(qbd, k_t) + cbias[g][j]
        s_n = _dot_nt(qbd, k_new) + bn_ref[g, j]
        m = jnp.maximum(jnp.max(s_c, axis=-1, keepdims=True), jnp.max(s_n, axis=-1, keepdims=True))
        p_c = jnp.exp(s_c - m)
        p_n = jnp.exp(s_n - m)
        den = jnp.sum(p_c, axis=-1, keepdims=True) + jnp.sum(p_n, axis=-1, keepdims=True)
        o = (_dot_nt(p_c.astype(BF16), v_t) + _dot(p_n.astype(BF16), v_new)) / den
        outs.append(per_head_rows(o))
        lses.append(per_head_rows(jnp.broadcast_to(m + jnp.log(den), o.shape)))
    m = jnp.maximum(jnp.maximum(lses[0], lses[1]), lses[2])
    es = [jnp.exp(l - m) for l in lses]
    num = es[0] * outs[0] + es[1] * outs[1] + es[2] * outs[2]
    merged = num / (es[0] + es[1] + es[2])
    for w in range(per_block):
        @pl.when(which == w)
        def _(w=w):
            o_ref[w * n_new:(w + 1) * n_new, :] = merged


def _sample_attention(q, kv, caches, cache_bias, new_bias, *, n_new, heads_per_step=N_HEADS):
    hs = heads_per_step
    width = hs * HEAD_DIM
    per_block = NEW_ROWS // n_new
    n_blocks = q.shape[0] // NEW_ROWS
    window = lambda col: pl.BlockSpec((NEW_ROWS, width), lambda r, j, w, col=col: (r, col + j))
    per_d = D_MODEL // width
    windows = []
    for g in range(N_BRANCH):
        windows += [window(g * per_d), window(2 * g * per_d), window((2 * g + 1) * per_d)]
    return pl.pallas_call(
        functools.partial(_sample_attn_kernel, n_new=n_new, heads=hs),
        grid=(n_blocks, N_HEADS // hs, per_block),
        in_specs=windows
        + [pl.BlockSpec((1, 2, hs, HEAD_DIM, c.shape[-1]), lambda r, j, w: (r * per_block + w, 0, j, 0, 0))
           for c in caches]
        + [_const_spec(t.shape) for t in cache_bias] + [_const_spec(new_bias.shape)],
        out_specs=pl.BlockSpec((NEW_ROWS, width), lambda r, j, w: (r, j)),
        out_shape=jax.ShapeDtypeStruct((q.shape[0], D_MODEL), F32),
        compiler_params=_params(3, SAMPLE_ATTN_VMEM_LIMIT_BYTES), name="sample_attention",
    )(*([q, kv, kv] * N_BRANCH), *caches, *cache_bias, new_bias)


def _t5_bucket(dist):
    max_exact = NUM_BUCKETS // 2
    n = jnp.maximum(dist, 1).astype(F32)
    large = max_exact + (jnp.log(n / max_exact) / math.log(MAX_DISTANCE / max_exact)
                         * (NUM_BUCKETS - max_exact)).astype(jnp.int32)
    large = jnp.minimum(large, NUM_BUCKETS - 1)
    return jnp.where(dist < max_exact, dist, large)


def _branch_bias(rel_bias, g):
    n_keys = WINDOWS[g] // DILATIONS[g] + 1
    dist = jnp.arange(n_keys, dtype=jnp.int32) * DILATIONS[g]
    return rel_bias[_t5_bucket(dist)][:, g * N_HEADS:(g + 1) * N_HEADS].T.astype(F32)


def _sample_bias_tables(biases, n_new, hs):
    neg = lambda *shape: jnp.full(shape, NEG, F32)
    by_step = lambda t: t.reshape(N_HEADS // hs, hs, n_new, -1).transpose(0, 2, 1, 3).reshape(
        N_HEADS // hs, n_new * hs, -1)
    cache_tabs, new_tabs = [], []
    for g, d in enumerate(DILATIONS):
        w = WINDOWS[g]
        n_steps = w // d
        b = biases[g]
        rows, new_rows = [], []
        for i in range(n_new):
            first = n_steps + i // d
            hit = b[:, min(first, n_steps):i // d:-1]
            hit = jnp.concatenate([neg(N_HEADS, n_steps - hit.shape[1]), hit], axis=1)
            grid = jnp.where((jnp.arange(d) == i % d)[None, None, :], hit[:, :, None], NEG)
            rows.append(grid.reshape(N_HEADS, w))
            cols = [b[:, (i - j) // d] if (j <= i and (i - j) % d == 0 and j < n_new) else neg(N_HEADS)
                    for j in range(NEW_ROWS)]
            new_rows.append(jnp.stack(cols, axis=1))
        cache_tabs.append(by_step(jnp.stack(rows, axis=1)))
        new_tabs.append(by_step(jnp.stack(new_rows, axis=1)))
    return cache_tabs, jnp.stack(new_tabs)


def kernel(x_prompt, x_sample, cache_kv_w128, cache_kv_w512, cache_kv_w2048, ln_g, ln_b, gm_w_in, gm_b_in,
           gm_ln_g, gm_ln_b, gm_w_s, gm_b_s, gm_w_out, gm_b_out, w_kv, attn_w_q, attn_w_o, rel_bias,
           mlp_w1, mlp_w2):
    batch, seq, _ = x_prompt.shape
    n_samp, n_new, _ = x_sample.shape
    ms = n_samp * n_new
    assert DEPTH == 2 and gm_w_in.shape[0] == 1 and attn_w_q.shape[0] == 1
    assert seq % Q_TILE == 0 and ms % CHUNK == 0 and CHUNK % n_new == 0 and NEW_ROWS % n_new == 0
    assert all(PAST_LEN >= w and w % d == 0 and w // d == KEY_BLOCK for w, d in zip(WINDOWS, DILATIONS))

    bf = lambda w: w.astype(BF16)
    vec = lambda v: v.reshape(1, -1)
    w_in, w_out = bf(gm_w_in[0]), bf(gm_w_out[0])
    wq_t = jnp.transpose(attn_w_q[0], (1, 2, 3, 0)).reshape(N_BRANCH * D_MODEL, D_MODEL)
    wkv_t = jnp.transpose(w_kv, (1, 2, 3, 4, 0)).reshape(N_BRANCH * 2 * D_MODEL, D_MODEL)
    gm_args = (vec(gm_b_in[0]), vec(gm_ln_g[0]), vec(gm_ln_b[0]))
    gm_tail = (w_out, vec(gm_b_out[0]), vec(ln_g[0, 0]), vec(ln_b[0, 0]))

    xp = x_prompt.reshape(batch * seq, D_MODEL)
    xs = x_sample.reshape(ms, D_MODEL)
    x1p, vn_p = _gmlp_layer(xp, w_in, *gm_args, gm_w_s[0], gm_b_s[0].T, *gm_tail, tm=GMLP_ROWS, rows_per_seq=seq)
    reps = CHUNK // n_new
    same_seq = (jnp.arange(CHUNK)[:, None] // n_new) == (jnp.arange(CHUNK)[None, :] // n_new)
    ws_s = jnp.where(same_seq[None], jnp.tile(gm_w_s[0][:, :n_new, :n_new], (1, reps, reps)), 0.0)
    bs_s = jnp.tile(gm_b_s[0][:, :n_new], (1, reps)).T
    x1s, vn_s = _gmlp_layer(xs, w_in, *gm_args, ws_s, bs_s, *gm_tail, tm=ms, rows_per_seq=None)

    x2s, w1b, w2b = _mlp_layer_rounding(x1s, mlp_w1, mlp_w2, 0, vec(ln_g[0, 1]), vec(ln_b[0, 1]))
    x2p = _mlp_layer(x1p, w1b, w2b, vec(ln_g[0, 1]), vec(ln_b[0, 1]), tm=MLP_ROWS)

    q_s, wq_tb = _project_rounding(x2s, wq_t, tn=ROUNDING_COLS)
    kv_s, wkv_tb = _project_rounding(x2s, wkv_t, tn=ROUNDING_COLS)
    q_p = _project(x2p, wq_tb, tm=PROJ_ROWS)
    kv_p = _project(x2p, wkv_tb, tm=PROJ_ROWS)

    biases = [_branch_bias(rel_bias, g) for g in range(N_BRANCH)]
    steps_rev = jnp.stack([b[:, :0:-1] for b in biases])[:, :, None, :]
    step0 = jnp.stack([jnp.broadcast_to(b[:, :1], (N_HEADS, LANES)) for b in biases])[:, :, None, :]
    o_p = _prompt_attention(q_p, kv_p, steps_rev, step0, batch=batch, seq=seq)

    caches = [jnp.transpose(c, (0, 2, 3, 4, 1)) for c in (cache_kv_w128, cache_kv_w512, cache_kv_w2048)]
    cache_bias, new_bias = _sample_bias_tables(biases, n_new, N_HEADS)
    o_s = _sample_attention(q_s, kv_s, caches, cache_bias, new_bias, n_new=n_new)

    attn_ln = (vec(ln_g[1, 0]), vec(ln_b[1, 0]))
    y_s, w1b, w2b, wob = _mlp_layer_rounding(x2s, mlp_w1, mlp_w2, 1, vec(ln_g[1, 1]), vec(ln_b[1, 1]),
                                             attn=(o_s, attn_w_o[0]) + attn_ln)
    y_p = _mlp_layer(x2p, w1b, w2b, vec(ln_g[1, 1]), vec(ln_b[1, 1]), tm=MLP_ROWS, attn=(o_p, wob) + attn_ln)

    wkv_t3 = wkv_tb.reshape(N_BRANCH, 2 * D_MODEL, D_MODEL)
    kv_p_out = []
    for g in range(N_BRANCH):
        rows = min(WINDOWS[g], seq)
        kv_t = _kv_state(x2p, wkv_t3, g, batch=batch, seq=seq, rows=rows, tw=min(rows, KV_STATE_COLS))
        kv_p_out.append(kv_t.reshape(batch, 2, N_HEADS, HEAD_DIM, rows).transpose(0, 4, 1, 2, 3))
    kv_s_t = _kv_state_sample(x2s.reshape(n_samp, n_new, D_MODEL).transpose(1, 0, 2), wkv_t3)
    kv_s_out = [kv_s_t[g].reshape(n_new, 2, N_HEADS, HEAD_DIM, n_samp).transpose(4, 0, 1, 2, 3)
                for g in range(N_BRANCH)]
    return (y_p.reshape(batch, seq, D_MODEL), y_s.reshape(n_samp, n_new, D_MODEL),
            vn_p[None], vn_s.reshape(1, n_samp, n_new, GATE),
            kv_p_out[0], kv_s_out[0], kv_p_out[1], kv_s_out[1], kv_p_out[2], kv_s_out[2])
```
